```python
import jax
import jax.numpy as jnp
from jax import lax
import numpy as np


D_MODEL = 1024
BATCH = 1
SEQ = 16384
DEPTH = 1
DEC_BATCH = 32
DEC_SEQ = 4
PAST_LEN = 16384
PAGE_SIZE = 128

N_META = 16
HEAD_DIM = 64
RW_HEADS = D_MODEL // HEAD_DIM
RW_WIDTH = RW_HEADS * HEAD_DIM
SB_HEADS = D_MODEL // HEAD_DIM
SB_WIDTH = SB_HEADS * HEAD_DIM
DECAY_LORA = 64
ICLR_LORA = 64
GATE_LORA = 128
SHIFT_WIDTH = 3 * RW_WIDTH + DECAY_LORA + ICLR_LORA + GATE_LORA
RW_SPLITS = (RW_WIDTH, 2 * RW_WIDTH, 3 * RW_WIDTH, 3 * RW_WIDTH + DECAY_LORA, 3 * RW_WIDTH + DECAY_LORA + ICLR_LORA)
PROJ_WIDTH = SHIFT_WIDTH + 3 * SB_WIDTH + RW_WIDTH + SB_WIDTH
Q_BLOCK = 128
SB_BIAS_INIT = -8.0
PEER_HEADS = 8
PEER_NKEYS = 128
PEER_QDIM = 256
PEER_HALF = PEER_QDIM // 2
PEER_TOPK = 16
N_EXPERTS = PEER_NKEYS * PEER_NKEYS
PEER_BLOCK = 128
NORM_EPS = 1e-6
GN_EPS = 64e-5

kernel_name = 'hybrid_rwkv7_stickbreak_peer_step'


def rmsnorm(x, g):
    xf = x.astype(jnp.float32)
    y = xf * lax.rsqrt(jnp.mean(xf * xf, axis=-1, keepdims=True) + NORM_EPS)
    return (y * g.astype(jnp.float32)).astype(x.dtype)


def rwkv7_mix(p_rw, shift0, s0, lw):
    b, t, _ = p_rw.shape
    f32 = jnp.float32
    prev = jnp.concatenate([shift0[:, None, :].astype(p_rw.dtype), p_rw[:, :-1]], axis=1)
    xs = (p_rw + lw['mu'] * (prev - p_rw)).astype(f32)
    r, k, v, xw, xa, xg = jnp.split(xs, RW_SPLITS, axis=-1)
    w = -jax.nn.softplus(-(lw['w0'].astype(f32) + jnp.tanh(xw) @ lw['w2'].astype(f32))) - 0.5
    decay = jnp.exp(-jnp.exp(w))
    a = jax.nn.sigmoid(lw['a0'].astype(f32) + xa @ lw['a2'].astype(f32))
    g = jax.nn.sigmoid(xg) @ lw['g2'].astype(f32)
    hd = lambda z: z.reshape(b, t, RW_HEADS, HEAD_DIM)
    r, k, v, decay, a = hd(r), hd(k), hd(v), hd(decay), hd(a)
    kk = k * lw['k_k'].astype(f32).reshape(RW_HEADS, HEAD_DIM)
    kk = kk * lax.rsqrt(jnp.maximum(jnp.sum(kk * kk, axis=-1, keepdims=True), 1e-12))
    k = k * (1.0 + (a - 1.0) * lw['k_a'].astype(f32).reshape(RW_HEADS, HEAD_DIM))

    def step(state, inp):
        r_t, w_t, k_t, v_t, kk_t, a_t = inp
        sa = jnp.einsum('bhij,bhj->bhi', state, -kk_t)
        state = (state * w_t[:, :, None, :] + sa[..., None] * (kk_t * a_t)[:, :, None, :]
                 + v_t[..., None] * k_t[:, :, None, :])
        return state, jnp.einsum('bhij,bhj->bhi', state, r_t)

    tm = lambda z: jnp.moveaxis(z, 1, 0)
    s_t, y = lax.scan(step, s0.astype(f32), (tm(r), tm(decay), tm(k), tm(v), tm(kk), tm(a)))
    y = jnp.moveaxis(y, 0, 1)
    mean = jnp.mean(y, axis=-1, keepdims=True)
    var = jnp.mean(jnp.square(y - mean), axis=-1, keepdims=True)
    y = ((y - mean) * lax.rsqrt(var + GN_EPS) * lw['gn_w'].astype(f32).reshape(RW_HEADS, HEAD_DIM)
         + lw['gn_b'].astype(f32).reshape(RW_HEADS, HEAD_DIM))
    y = y + jnp.sum(r * k * lw['r_k'].astype(f32), axis=-1, keepdims=True) * v
    o = (y.reshape(b, t, RW_WIDTH) * g).astype(p_rw.dtype)
    return o, p_rw[:, -1], s_t.astype(s0.dtype)


def sb_weights(z, mask):
    log_1m = jnp.where(mask, jax.nn.log_sigmoid(-z), 0.0)
    after = lax.cumsum(log_1m, axis=z.ndim - 1, reverse=True) - log_1m
    return jnp.where(mask, jnp.exp(jax.nn.log_sigmoid(z) + after), 0.0)


def sb_block(qb, q_pos, kb, vb, k_pos, bias):
    z = (jnp.einsum('bqhd,bkhd->bhqk', qb.astype(jnp.float32), kb.astype(jnp.float32)) * (HEAD_DIM ** -0.5)
         + bias.astype(jnp.float32)[None, :, None, None])
    a = sb_weights(z, k_pos[None, :] < q_pos[:, None])
    return jnp.einsum('bhqk,bkhd->bqhd', a.astype(vb.dtype), vb)


def sb_prompt(q, k, v, bias):
    b, t = q.shape[0], q.shape[1]
    meta_pos = jnp.arange(N_META)
    o_meta = sb_block(q[:, :N_META], meta_pos, k[:, :N_META], v[:, :N_META], meta_pos, bias)
    n_blk = (t - N_META) // Q_BLOCK
    qb = jnp.moveaxis(q[:, N_META:].reshape(b, n_blk, Q_BLOCK, SB_HEADS, HEAD_DIM), 1, 0)
    k_pos = jnp.arange(t)

    def one_block(args):
        q_i, i = args
        q_pos = N_META + i * Q_BLOCK + jnp.arange(Q_BLOCK)
        return sb_block(q_i, q_pos, k, v, k_pos, bias)

    o_real = lax.map(one_block, (qb, jnp.arange(n_blk)))
    o_real = jnp.moveaxis(o_real, 0, 1).reshape(b, t - N_META, SB_HEADS, HEAD_DIM)
    return jnp.concatenate([o_meta, o_real], axis=1)


def sb_sample(q, k_new, v_new, k_past, v_past, bias):
    f32 = jnp.float32
    db, s = q.shape[0], q.shape[1]
    n_pages, psize = k_past.shape[1], k_past.shape[2]
    past_len = n_pages * psize
    qf = q.astype(f32) * (HEAD_DIM ** -0.5)
    z_past = jnp.einsum('bqhd,bnphd->bhqnp', qf, k_past.astype(f32)).reshape(db, SB_HEADS, s, past_len)
    z_new = jnp.einsum('bqhd,bkhd->bhqk', qf, k_new.astype(f32))
    z = jnp.concatenate([z_past, z_new], axis=-1) + bias.astype(f32)[None, :, None, None]
    mask = jnp.concatenate([jnp.ones((s, past_len), bool), jnp.arange(s)[None, :] < jnp.arange(s)[:, None]], axis=-1)
    a = sb_weights(z, mask)
    a_past = a[..., :past_len].reshape(db, SB_HEADS, s, n_pages, psize).astype(v_past.dtype)
    return (jnp.einsum('bhqnp,bnphd->bqhd', a_past, v_past)
            + jnp.einsum('bhqk,bkhd->bqhd', a[..., past_len:].astype(v_new.dtype), v_new))


def token_mixer(xn, shift0, s0, lw, attend):
    b, t, _ = xn.shape
    p = xn @ lw['w_in']
    p_rw, p_sb, p_gate = jnp.split(p, [SHIFT_WIDTH, SHIFT_WIDTH + 3 * SB_WIDTH], axis=-1)
    o_a, shift_t, s_t = rwkv7_mix(p_rw, shift0, s0, lw)
    q, k, v = [z.reshape(b, t, SB_HEADS, HEAD_DIM) for z in jnp.split(p_sb, 3, axis=-1)]
    q = rmsnorm(q, lw['q_g'])
    k = rmsnorm(k, lw['k_g'])
    o_b = attend(q, k, v, lw['sb_bias']).reshape(b, t, SB_WIDTH)
    g_a, g_b = jnp.split(p_gate, 2, axis=-1)
    merged = jax.nn.sigmoid(g_a) * o_a + jax.nn.sigmoid(g_b) * o_b
    return merged @ lw['w_out'], shift_t, s_t, k, v


def peer_ffn(x, wq, sk1, sk2, u_tab, v_tab):
    b, t, d = x.shape
    f32 = jnp.float32
    n = b * t
    xf = jnp.pad(x.reshape(n, d), ((0, (-n) % PEER_BLOCK), (0, 0)))
    xb = xf.reshape(-1, PEER_BLOCK, d)

    def one_block(xi):
        q = (xi @ wq).reshape(PEER_BLOCK, PEER_HEADS, PEER_QDIM).astype(f32)
        s1 = jnp.einsum('nhd,hkd->nhk', q[..., :PEER_HALF], sk1.astype(f32))
        s2 = jnp.einsum('nhd,hkd->nhk', q[..., PEER_HALF:], sk2.astype(f32))
        v1, i1 = lax.top_k(s1, PEER_TOPK)
        v2, i2 = lax.top_k(s2, PEER_TOPK)
        cand = (v1[..., :, None] + v2[..., None, :]).reshape(PEER_BLOCK, PEER_HEADS, PEER_TOPK * PEER_TOPK)
        sc, ci = lax.top_k(cand, PEER_TOPK)
        e = (jnp.take_along_axis(i1, ci // PEER_TOPK, axis=-1) * PEER_NKEYS
             + jnp.take_along_axis(i2, ci % PEER_TOPK, axis=-1))
        gate = jax.nn.softmax(sc, axis=-1)
        act = jax.nn.gelu(jnp.einsum('nd,nhkd->nhk', xi.astype(f32), u_tab[e].astype(f32)), approximate=False)
        return jnp.einsum('nhk,nhkd->nd', (gate * act).astype(x.dtype), v_tab[e])

    y = lax.map(one_block, xb).reshape(-1, d)[:n]
    return y.reshape(b, t, d)


def setup_inputs(seed: int = 0) -> dict:
    key = jax.random.key(seed)
    ks = iter(jax.random.split(key, 40))
    f32 = jnp.float32
    normal = lambda shape, scale: jax.random.normal(next(ks), shape, f32) * scale
    n_pages = PAST_LEN // PAGE_SIZE
    used = DEC_BATCH * n_pages
    n_phys = used + max(1, used // 4)
    return {
        'x_prompt': normal((BATCH, SEQ, D_MODEL), 1.0),
        'x_sample': normal((DEC_BATCH, DEC_SEQ, D_MODEL), 1.0),
        'cache_k': normal((DEPTH, n_phys, PAGE_SIZE, SB_HEADS, HEAD_DIM), 1.0),
        'cache_v': normal((DEPTH, n_phys, PAGE_SIZE, SB_HEADS, HEAD_DIM), 1.0),
        'page_table': jax.random.permutation(next(ks), n_phys)[:used].reshape(DEC_BATCH, n_pages).astype(jnp.int32),
        'state_shift': normal((DEPTH, DEC_BATCH, SHIFT_WIDTH), 1.0),
        'state_wkv': normal((DEPTH, DEC_BATCH, RW_HEADS, HEAD_DIM, HEAD_DIM), 0.5),
        'meta_tokens': normal((N_META, D_MODEL), 1.0),
        'norm_mix_g': 1.0 + normal((DEPTH, D_MODEL), 0.02),
        'w_in': normal((DEPTH, D_MODEL, PROJ_WIDTH), D_MODEL ** -0.5),
        'shift_mu': jax.random.uniform(next(ks), (DEPTH, SHIFT_WIDTH), f32),
        'decay_w0': jax.random.uniform(next(ks), (DEPTH, RW_WIDTH), f32, -6.0, 0.0),
        'decay_w2': normal((DEPTH, DECAY_LORA, RW_WIDTH), 0.1),
        'iclr_a0': normal((DEPTH, RW_WIDTH), 0.1),
        'iclr_a2': normal((DEPTH, ICLR_LORA, RW_WIDTH), 0.1),
        'gate_g2': normal((DEPTH, GATE_LORA, RW_WIDTH), GATE_LORA ** -0.5),
        'k_k': 0.85 + normal((DEPTH, RW_WIDTH), 0.05),
        'k_a': 1.0 + normal((DEPTH, RW_WIDTH), 0.05),
        'r_k': normal((DEPTH, RW_HEADS, HEAD_DIM), 0.1),
        'gn_w': 1.0 + normal((DEPTH, RW_WIDTH), 0.02),
        'gn_b': normal((DEPTH, RW_WIDTH), 0.02),
        'q_norm_g': 1.0 + normal((DEPTH, HEAD_DIM), 0.02),
        'k_norm_g': 1.0 + normal((DEPTH, HEAD_DIM), 0.02),
        'sb_bias': SB_BIAS_INIT + normal((DEPTH, SB_HEADS), 0.5),
        'w_out': normal((DEPTH, D_MODEL, D_MODEL), D_MODEL ** -0.5),
        'norm_ffn_g': 1.0 + normal((DEPTH, D_MODEL), 0.02),
        'peer_wq': normal((DEPTH, D_MODEL, PEER_HEADS * PEER_QDIM), D_MODEL ** -0.5),
        'peer_subkeys1': normal((DEPTH, PEER_HEADS, PEER_NKEYS, PEER_HALF), PEER_HALF ** -0.5),
        'peer_subkeys2': normal((DEPTH, PEER_HEADS, PEER_NKEYS, PEER_HALF), PEER_HALF ** -0.5),
        'peer_u': normal((DEPTH, N_EXPERTS, D_MODEL), D_MODEL ** -0.5),
        'peer_v': normal((DEPTH, N_EXPERTS, D_MODEL), D_MODEL ** -0.5),
    }


def reference(x_prompt, x_sample, cache_k, cache_v, page_table, state_shift, state_wkv, meta_tokens,
              norm_mix_g, w_in, shift_mu, decay_w0, decay_w2, iclr_a0, iclr_a2, gate_g2, k_k, k_a, r_k,
              gn_w, gn_b, q_norm_g, k_norm_g, sb_bias, w_out, norm_ffn_g, peer_wq, peer_subkeys1, peer_subkeys2,
              peer_u, peer_v):
    b = x_prompt.shape[0]
    xp = jnp.concatenate([jnp.broadcast_to(meta_tokens[None].astype(x_prompt.dtype), (b, N_META, D_MODEL)),
                          x_prompt], axis=1)
    xs = x_sample
    kp_l, vp_l, shp_l, wkvp_l, ks_l, vs_l, shs_l, wkvs_l = [], [], [], [], [], [], [], []
    for l in range(DEPTH):
        lw = {'w_in': w_in[l], 'mu': shift_mu[l], 'w0': decay_w0[l], 'w2': decay_w2[l], 'a0': iclr_a0[l],
              'a2': iclr_a2[l], 'g2': gate_g2[l], 'k_k': k_k[l], 'k_a': k_a[l], 'r_k': r_k[l], 'gn_w': gn_w[l],
              'gn_b': gn_b[l], 'q_g': q_norm_g[l], 'k_g': k_norm_g[l], 'sb_bias': sb_bias[l], 'w_out': w_out[l]}
        o, sh, st, kr, vr = token_mixer(rmsnorm(xp, norm_mix_g[l]), jnp.zeros((b, SHIFT_WIDTH), xp.dtype),
                                        jnp.zeros((b, RW_HEADS, HEAD_DIM, HEAD_DIM), xp.dtype), lw, sb_prompt)
        xp = xp + o
        kp_l.append(kr); vp_l.append(vr); shp_l.append(sh); wkvp_l.append(st)
        if l == DEPTH - 1:
            xp = xp[:, N_META:]
        xp = xp + peer_ffn(rmsnorm(xp, norm_ffn_g[l]), peer_wq[l], peer_subkeys1[l], peer_subkeys2[l],
                           peer_u[l], peer_v[l])
        k_past = cache_k[l][page_table]
        v_past = cache_v[l][page_table]
        attend_s = lambda q, k, v, bias, kp=k_past, vp=v_past: sb_sample(q, k, v, kp, vp, bias)
        o, sh, st, kr, vr = token_mixer(rmsnorm(xs, norm_mix_g[l]), state_shift[l], state_wkv[l], lw, attend_s)
        xs = xs + o
        ks_l.append(kr); vs_l.append(vr); shs_l.append(sh); wkvs_l.append(st)
        xs = xs + peer_ffn(rmsnorm(xs, norm_ffn_g[l]), peer_wq[l], peer_subkeys1[l], peer_subkeys2[l],
                           peer_u[l], peer_v[l])
    return (xp, xs, jnp.stack(kp_l), jnp.stack(vp_l), jnp.stack(shp_l), jnp.stack(wkvp_l),
            jnp.stack(ks_l), jnp.stack(vs_l), jnp.stack(shs_l), jnp.stack(wkvs_l))
```

```python
import functools

import jax
import jax.numpy as jnp
from jax import lax
from jax.experimental import pallas as pl
from jax.experimental.pallas import tpu as pltpu

F32 = jnp.float32
BF16 = jnp.bfloat16
HIGHEST = lax.Precision.HIGHEST

D_MODEL = 1024
HEAD_DIM = 64
N_HEADS = 16
N_META = 16
RW_WIDTH = 1024
LORA_W = 64
GATE_W = 128
SHIFT_WIDTH = 3 * RW_WIDTH + 2 * LORA_W + GATE_W
NORM_EPS = 1e-6
GN_EPS = 64e-5
PEER_HEADS = 8
PEER_NKEYS = 128
PEER_HALF = 128
PEER_TOPK = 16
SEG_BLOCK = 256
VMEM_LIMIT = 56 * 1024 * 1024


def _cparams(sem):
    return pltpu.CompilerParams(dimension_semantics=sem, vmem_limit_bytes=VMEM_LIMIT)


def _seg_ones():
    r = jnp.arange(SEG_BLOCK) // HEAD_DIM
    return (r[:, None] == r[None, :]).astype(F32)


def _seg_sum(x, bd):
    parts = []
    for c in range(x.shape[1] // SEG_BLOCK):
        parts.append(jnp.dot(x[:, c * SEG_BLOCK:(c + 1) * SEG_BLOCK], bd, precision=HIGHEST,
                             preferred_element_type=F32))
    return jnp.concatenate(parts, axis=1)


def _rms_rows(x, g):
    return x * lax.rsqrt(jnp.mean(x * x, axis=-1, keepdims=True) + NORM_EPS) * g


def _softplus(x):
    return jnp.maximum(x, 0.0) + jnp.log(1.0 + jnp.exp(-jnp.abs(x)))


def _sigmoid(x):
    return 1.0 / (1.0 + jnp.exp(-x))


def _proj_rw_kernel(x_ref, g_ref, w_ref, o_ref):
    xn = _rms_rows(x_ref[...], g_ref[...])
    o_ref[...] = jnp.dot(xn.astype(BF16), w_ref[...], preferred_element_type=F32)


def _proj_rw(x, g, w_bf, tm):
    t = x.shape[0]
    n = w_bf.shape[1]
    return pl.pallas_call(
        _proj_rw_kernel,
        grid=(t // tm,),
        in_specs=[pl.BlockSpec((tm, D_MODEL), lambda i: (i, 0)),
                  pl.BlockSpec((1, D_MODEL), lambda i: (0, 0)),
                  pl.BlockSpec((D_MODEL, n), lambda i: (0, 0))],
        out_specs=pl.BlockSpec((tm, n), lambda i: (i, 0)),
        out_shape=jax.ShapeDtypeStruct((t, n), F32),
        compiler_params=_cparams(("parallel",)),
        name="proj_rw",
    )(x, g, w_bf)


def _proj_sb_kernel(x_ref, g_ref, w_ref, qg_ref, kg_ref, bd_ref,
                    q_ref, k_ref, kb_ref, v_ref, vb_ref, gs_ref):
    xn = _rms_rows(x_ref[...], g_ref[...]).astype(BF16)
    p = jnp.dot(xn, w_ref[...], preferred_element_type=F32)
    bd = bd_ref[...]
    q = p[:, 0:D_MODEL]
    k = p[:, D_MODEL:2 * D_MODEL]
    v = p[:, 2 * D_MODEL:3 * D_MODEL]
    qn = q * lax.rsqrt(_seg_sum(q * q, bd) * (1.0 / HEAD_DIM) + NORM_EPS) * qg_ref[...]
    kn = k * lax.rsqrt(_seg_sum(k * k, bd) * (1.0 / HEAD_DIM) + NORM_EPS) * kg_ref[...]
    q_ref[...] = (qn * (HEAD_DIM ** -0.5)).astype(BF16)
    k_ref[...] = kn
    kb_ref[...] = kn.astype(BF16)
    v_ref[...] = v
    vb_ref[...] = v.astype(BF16)
    gs_ref[...] = _sigmoid(p[:, 3 * D_MODEL:5 * D_MODEL])


def _proj_sb(x, g, w_bf, qg, kg, bd, tm):
    t = x.shape[0]
    n = w_bf.shape[1]
    row = lambda w: pl.BlockSpec((tm, w), lambda i: (i, 0))
    full = lambda a: pl.BlockSpec(a.shape, lambda i: (0,) * a.ndim)
    return pl.pallas_call(
        _proj_sb_kernel,
        grid=(t // tm,),
        in_specs=[row(D_MODEL), full(g), full(w_bf), full(qg), full(kg), full(bd)],
        out_specs=[row(D_MODEL), row(D_MODEL), row(D_MODEL), row(D_MODEL), row(D_MODEL), row(2 * D_MODEL)],
        out_shape=[jax.ShapeDtypeStruct((t, D_MODEL), BF16), jax.ShapeDtypeStruct((t, D_MODEL), F32),
                   jax.ShapeDtypeStruct((t, D_MODEL), BF16), jax.ShapeDtypeStruct((t, D_MODEL), F32),
                   jax.ShapeDtypeStruct((t, D_MODEL), BF16), jax.ShapeDtypeStruct((t, 2 * D_MODEL), F32)],
        compiler_params=_cparams(("parallel",)),
        name="proj_sb",
    )(x, g, w_bf, qg, kg, bd)


def _rwkv_kernel(p_ref, sh0_ref, s0_ref, mu_ref, w0_ref, w2_ref, a0_ref, a2_ref, g2_ref, kk_ref, ka_ref,
                 rk_ref, gnw_ref, gnb_ref, bd_ref,
                 o_ref, st_out_ref,
                 st_ref, last_ref, r_s, w_s, k_s, v_s, kk_s, b_s, y_s, *, n_steps):
    c = pl.program_id(1)
    tc = p_ref.shape[1]

    @pl.when(c == 0)
    def _():
        st_ref[...] = s0_ref[0]
        last_ref[...] = sh0_ref[0]

    p = p_ref[0]
    row = lax.broadcasted_iota(jnp.int32, p.shape, 0)
    prev = jnp.where(row == 0, last_ref[...], pltpu.roll(p, 1, axis=0))
    last_ref[...] = p[n_steps - 1:n_steps, :]
    xs = p + mu_ref[...] * (prev - p)
    r = xs[:, 0:RW_WIDTH]
    k = xs[:, RW_WIDTH:2 * RW_WIDTH]
    v = xs[:, 2 * RW_WIDTH:3 * RW_WIDTH]
    o1 = 3 * RW_WIDTH
    xw = xs[:, o1:o1 + LORA_W]
    xa = xs[:, o1 + LORA_W:o1 + 2 * LORA_W]
    xg = xs[:, o1 + 2 * LORA_W:o1 + 2 * LORA_W + GATE_W]
    bd = bd_ref[...]
    dot_hi = functools.partial(jnp.dot, precision=HIGHEST, preferred_element_type=F32)
    w = -_softplus(-(w0_ref[...] + dot_hi(jnp.tanh(xw), w2_ref[...]))) - 0.5
    decay = jnp.exp(-jnp.exp(w))
    a = _sigmoid(a0_ref[...] + dot_hi(xa, a2_ref[...]))
    g = dot_hi(_sigmoid(xg), g2_ref[...])
    kk = k * kk_ref[...]
    kk = kk * lax.rsqrt(jnp.maximum(_seg_sum(kk * kk, bd), 1e-12))
    k = k * (1.0 + (a - 1.0) * ka_ref[...])
    b = kk * a
    for h in range(N_HEADS):
        sl = slice(h * HEAD_DIM, (h + 1) * HEAD_DIM)
        r_s[h] = r[:, sl]
        w_s[h] = decay[:, sl]
        k_s[h] = k[:, sl]
        v_s[h] = v[:, sl]
        kk_s[h] = kk[:, sl]
        b_s[h] = b[:, sl]
    if n_steps < tc:
        y_s[...] = jnp.zeros_like(y_s)

    eye = (lax.broadcasted_iota(jnp.int32, (HEAD_DIM, HEAD_DIM), 0)
           == lax.broadcasted_iota(jnp.int32, (HEAD_DIM, HEAD_DIM), 1)).astype(F32)

    def step(t, carry):
        for h in range(N_HEADS):
            s = st_ref[h]
            kk_t = kk_s[h, pl.ds(t, 1), :]
            v_col = jnp.sum(eye * v_s[h, pl.ds(t, 1), :], axis=1, keepdims=True)
            sa = -jnp.sum(s * kk_t, axis=1, keepdims=True)
            s = (s * w_s[h, pl.ds(t, 1), :] + sa * b_s[h, pl.ds(t, 1), :]
                 + v_col * k_s[h, pl.ds(t, 1), :])
            st_ref[h] = s
            y_col = jnp.sum(s * r_s[h, pl.ds(t, 1), :], axis=1, keepdims=True)
            y_s[h, pl.ds(t, 1), :] = jnp.sum(eye * y_col, axis=0, keepdims=True)
        return carry

    lax.fori_loop(0, n_steps, step, 0)

    y = jnp.concatenate([y_s[h] for h in range(N_HEADS)], axis=1)
    mean = _seg_sum(y, bd) * (1.0 / HEAD_DIM)
    yc = y - mean
    var = _seg_sum(yc * yc, bd) * (1.0 / HEAD_DIM)
    y = yc * lax.rsqrt(var + GN_EPS) * gnw_ref[...] + gnb_ref[...]
    y = y + _seg_sum(r * k * rk_ref[...], bd) * v
    o_ref[0] = y * g

    @pl.when(c == pl.num_programs(1) - 1)
    def _():
        st_out_ref[0] = st_ref[...]


def _rwkv(p_rw, shift0, s0, lw, bd, tc, n_steps, skip_chunks=0):
    b, t, _ = p_rw.shape
    nc = t // tc - skip_chunks
    vec = lambda a: a.reshape(1, -1).astype(F32)
    params = [vec(lw['mu']), vec(lw['w0']), lw['w2'], vec(lw['a0']), lw['a2'], lw['g2'], vec(lw['k_k']),
              vec(lw['k_a']), vec(lw['r_k']), vec(lw['gn_w']), vec(lw['gn_b']), bd]
    full = lambda a: pl.BlockSpec(a.shape, lambda i, c: (0,) * a.ndim)
    head_scratch = pltpu.VMEM((N_HEADS, tc, HEAD_DIM), F32)
    return pl.pallas_call(
        functools.partial(_rwkv_kernel, n_steps=n_steps),
        grid=(b, nc),
        in_specs=[pl.BlockSpec((1, tc, SHIFT_WIDTH), lambda i, c: (i, c + skip_chunks, 0)),
                  pl.BlockSpec((1, 1, SHIFT_WIDTH), lambda i, c: (i, 0, 0)),
                  pl.BlockSpec((1, N_HEADS, HEAD_DIM, HEAD_DIM), lambda i, c: (i, 0, 0, 0))]
                 + [full(a) for a in params],
        out_specs=[pl.BlockSpec((1, tc, RW_WIDTH), lambda i, c: (i, c + skip_chunks, 0)),
                   pl.BlockSpec((1, N_HEADS, HEAD_DIM, HEAD_DIM), lambda i, c: (i, 0, 0, 0))],
        out_shape=[jax.ShapeDtypeStruct((b, t, RW_WIDTH), F32),
                   jax.ShapeDtypeStruct((b, N_HEADS, HEAD_DIM, HEAD_DIM), F32)],
        scratch_shapes=[pltpu.VMEM((N_HEADS, HEAD_DIM, HEAD_DIM), F32),
                        pltpu.VMEM((1, SHIFT_WIDTH), F32)] + [head_scratch] * 7,
        compiler_params=_cparams(("parallel", "arbitrary")),
        name="rwkv7",
    )(p_rw, shift0, s0, *params)


def _sb_prompt_kernel(q_ref, k_ref, v_ref, b_ref, tri_ref, o_ref, acc_ref, *, tq, tk, pad):
    i = pl.program_id(1)
    q = q_ref[...]
    lane = lax.broadcasted_iota(jnp.int32, (1, 2 * HEAD_DIM), 1)
    halves = (lane < HEAD_DIM, lane >= HEAD_DIM)
    qh = [jnp.where(m, q, jnp.zeros_like(q)) for m in halves]
    qpos = i * tq + lax.broadcasted_iota(jnp.int32, (tq, 1), 0)
    tri = tri_ref[...]
    j_hi = ((i + 1) * tq - 2) // tk
    j_lo = pad // tk
    acc_ref[...] = jnp.zeros_like(acc_ref)

    def body(jj, carry):
        j = j_hi - jj
        off = pl.multiple_of(j * tk, tk)
        ks = k_ref[pl.ds(off, tk), :]
        vs = v_ref[pl.ds(off, tk), :]
        kpos = j * tk + lax.broadcasted_iota(jnp.int32, (1, tk), 1)
        mask = jnp.logical_and(kpos < qpos, kpos >= pad)
        new = []
        for h in range(2):
            z = lax.dot_general(qh[h], ks, (((1,), (1,)), ((), ())), preferred_element_type=F32)
            z = z + b_ref[h][:, 0:1]
            sp = _softplus(z)
            lg = jnp.where(mask, -sp, 0.0)
            hi = lg.astype(BF16)
            lo = (lg - hi.astype(F32)).astype(BF16)
            after = (jnp.dot(hi, tri, preferred_element_type=F32)
                     + jnp.dot(lo, tri, preferred_element_type=F32))
            a = jnp.where(mask, jnp.exp((z - sp) + after + carry[h]), 0.0)
            vh = jnp.where(halves[h], vs, jnp.zeros_like(vs))
            acc_ref[...] += jnp.dot(a.astype(BF16), vh, preferred_element_type=F32)
            new.append(carry[h] + jnp.sum(lg, axis=1, keepdims=True))
        return tuple(new)

    zero = jnp.zeros((tq, 1), F32)
    lax.fori_loop(0, j_hi - j_lo + 1, body, (zero, zero))
    o_ref[...] = acc_ref[...]


def _sb_prompt(qb, kb, vb, bias, tq, tk, pad):
    t = qb.shape[0]
    n_pairs = N_HEADS // 2
    tri = (jnp.arange(tk)[:, None] > jnp.arange(tk)[None, :]).astype(BF16)
    bias_l = jnp.broadcast_to(bias.astype(F32).reshape(N_HEADS, 1, 1), (N_HEADS, 1, 2 * HEAD_DIM))
    return pl.pallas_call(
        functools.partial(_sb_prompt_kernel, tq=tq, tk=tk, pad=pad),
        grid=(n_pairs, t // tq),
        in_specs=[pl.BlockSpec((tq, 2 * HEAD_DIM), lambda p, i: (i, p)),
                  pl.BlockSpec((t, 2 * HEAD_DIM), lambda p, i: (0, p)),
                  pl.BlockSpec((t, 2 * HEAD_DIM), lambda p, i: (0, p)),
                  pl.BlockSpec((2, 1, 2 * HEAD_DIM), lambda p, i: (p, 0, 0)),
                  pl.BlockSpec((tk, tk), lambda p, i: (0, 0))],
        out_specs=pl.BlockSpec((tq, 2 * HEAD_DIM), lambda p, i: (i, p)),
        out_shape=jax.ShapeDtypeStruct((t, D_MODEL), F32),
        scratch_shapes=[pltpu.VMEM((tq, 2 * HEAD_DIM), F32)],
        compiler_params=_cparams(("parallel", "parallel")),
        name="sb_prompt",
    )(qb, kb, vb, bias_l, tri)


def _sb_sample_kernel(pt_ref, qbd_ref, kn_ref, vn_ref, kc_ref, vc_ref, b_ref, tri_ref, dmask_ref,
                      o_ref, acc_ref, carry_ref, *, s_new):
    del pt_ref
    step = pl.program_id(1)
    qbd = qbd_ref[0]
    bias = b_ref[...]
    tri = tri_ref[...]
    nt = (((1,), (1,)), ((), ()))

    def block(keys, vals, mask):
        z = lax.dot_general(qbd, keys.astype(BF16), nt, preferred_element_type=F32) + bias
        sp = _softplus(z)
        lg = -sp if mask is None else jnp.where(mask, -sp, 0.0)
        after = jnp.dot(lg, tri, precision=HIGHEST, preferred_element_type=F32)
        a = jnp.exp((z - sp) + after + carry_ref[...])
        if mask is not None:
            a = jnp.where(mask, a, 0.0)
        acc_ref[...] += jnp.dot(a.astype(BF16), vals.astype(BF16), preferred_element_type=F32)
        carry_ref[...] += jnp.sum(lg, axis=1, keepdims=True)

    @pl.when(step == 0)
    def _():
        acc_ref[...] = jnp.zeros_like(acc_ref)
        carry_ref[...] = jnp.zeros_like(carry_ref)
        shape = (qbd.shape[0], kn_ref.shape[1])
        kidx = lax.broadcasted_iota(jnp.int32, shape, 1)
        qidx = lax.broadcasted_iota(jnp.int32, shape, 0) // N_HEADS
        block(kn_ref[0], vn_ref[0], kidx < qidx)

    block(kc_ref[0], vc_ref[0], None)

    @pl.when(step == pl.num_programs(1) - 1)
    def _():
        sel = acc_ref[...] * dmask_ref[...]
        for qi in range(s_new):
            o_ref[0, qi:qi + 1, :] = jnp.sum(sel[qi * N_HEADS:(qi + 1) * N_HEADS, :], axis=0, keepdims=True)


def _sb_sample(qn, kn, v, cache_k, cache_v, page_table, bias):
    db, s_new, _ = qn.shape
    n_phys, psize = cache_k.shape[0], cache_k.shape[1]
    n_pages = page_table.shape[1]
    ncol = s_new * N_HEADS
    n_pad = psize
    kc = cache_k.reshape(n_phys, psize, D_MODEL)
    vc = cache_v.reshape(n_phys, psize, D_MODEL)
    head_of_lane = jnp.arange(D_MODEL) // HEAD_DIM
    col_head = jnp.arange(ncol) % N_HEADS
    dmask = (col_head[:, None] == head_of_lane[None, :]).astype(F32)
    qbd = (jnp.repeat(qn, N_HEADS, axis=1) * dmask[None]).astype(BF16)
    padn = lambda a: jnp.pad(a, ((0, 0), (0, n_pad - s_new), (0, 0)))
    bias_c = bias.astype(F32)[col_head].reshape(ncol, 1)
    tri = (jnp.arange(psize)[:, None] > jnp.arange(psize)[None, :]).astype(F32)
    grid_spec = pltpu.PrefetchScalarGridSpec(
        num_scalar_prefetch=1,
        grid=(db, n_pages),
        in_specs=[pl.BlockSpec((1, ncol, D_MODEL), lambda b, j, pt: (b, 0, 0)),
                  pl.BlockSpec((1, n_pad, D_MODEL), lambda b, j, pt: (b, 0, 0)),
                  pl.BlockSpec((1, n_pad, D_MODEL), lambda b, j, pt: (b, 0, 0)),
                  pl.BlockSpec((1, psize, D_MODEL), lambda b, j, pt: (pt[b, n_pages - 1 - j], 0, 0)),
                  pl.BlockSpec((1, psize, D_MODEL), lambda b, j, pt: (pt[b, n_pages - 1 - j], 0, 0)),
                  pl.BlockSpec((ncol, 1), lambda b, j, pt: (0, 0)),
                  pl.BlockSpec((psize, psize), lambda b, j, pt: (0, 0)),
                  pl.BlockSpec((ncol, D_MODEL), lambda b, j, pt: (0, 0))],
        out_specs=pl.BlockSpec((1, s_new, D_MODEL), lambda b, j, pt: (b, 0, 0)),
        scratch_shapes=[pltpu.VMEM((ncol, D_MODEL), F32), pltpu.VMEM((ncol, 1), F32)],
    )
    return pl.pallas_call(
        functools.partial(_sb_sample_kernel, s_new=s_new),
        grid_spec=grid_spec,
        out_shape=jax.ShapeDtypeStruct((db, s_new, D_MODEL), F32),
        compiler_params=_cparams(("parallel", "arbitrary")),
        name="sb_sample",
    )(page_table, qbd, padn(kn), padn(v), kc, vc, bias_c, tri, dmask)


def _merge_kernel(x_ref, oa_ref, ob_ref, gs_ref, w_ref, o_ref):
    gs = gs_ref[...]
    merged = gs[:, 0:D_MODEL] * oa_ref[...] + gs[:, D_MODEL:2 * D_MODEL] * ob_ref[...]
    o_ref[...] = x_ref[...] + jnp.dot(merged.astype(BF16), w_ref[...], preferred_element_type=F32)


def _merge(x, oa, ob, gs, w_bf, tm):
    t = x.shape[0]
    row = lambda w: pl.BlockSpec((tm, w), lambda i: (i, 0))
    return pl.pallas_call(
        _merge_kernel,
        grid=(t // tm,),
        in_specs=[row(D_MODEL), row(D_MODEL), row(D_MODEL), row(2 * D_MODEL),
                  pl.BlockSpec((D_MODEL, D_MODEL), lambda i: (0, 0))],
        out_specs=row(D_MODEL),
        out_shape=jax.ShapeDtypeStruct((t, D_MODEL), F32),
        compiler_params=_cparams(("parallel",)),
        name="merge_out",
    )(x, oa, ob, gs, w_bf)


def _top16(s):
    n = s.shape[0]
    row = lax.broadcasted_iota(jnp.int32, s.shape, 0).astype(F32)
    rank = jnp.full(s.shape, float(PEER_TOPK), F32)
    vals = []
    for it in range(PEER_TOPK):
        m = jnp.max(s, axis=0, keepdims=True)
        first = jnp.min(jnp.where(s == m, row, float(n)), axis=0, keepdims=True)
        hit = row == first
        rank = jnp.where(hit, float(it), rank)
        s = jnp.where(hit, -jnp.inf, s)
        vals.append(m)
    return jnp.concatenate(vals, axis=0), rank


def _peer_route_kernel(h_ref, g_ref, wq_ref, sk1_ref, sk2_ref,
                       xn_ref, c1_ref, e1_ref, r2_ref, e2_ref):
    xn = _rms_rows(h_ref[...], g_ref[...])
    xn_ref[...] = xn.astype(BF16)
    q = jnp.dot(xn, wq_ref[...], precision=HIGHEST, preferred_element_type=F32)
    nt = (((1,), (1,)), ((), ()))
    for h in range(PEER_HEADS):
        q1 = q[:, h * 2 * PEER_HALF:h * 2 * PEER_HALF + PEER_HALF]
        q2 = q[:, h * 2 * PEER_HALF + PEER_HALF:(h + 1) * 2 * PEER_HALF]
        s1 = lax.dot_general(sk1_ref[h], q1, nt, precision=HIGHEST, preferred_element_type=F32)
        s2 = lax.dot_general(sk2_ref[h], q2, nt, precision=HIGHEST, preferred_element_type=F32)
        v1, rank1 = _top16(s1)
        v2, rank2 = _top16(s2)
        cand = jnp.concatenate([v1[a:a + 1, :] + v2 for a in range(PEER_TOPK)], axis=0)
        crow = lax.broadcasted_iota(jnp.int32, cand.shape, 0).astype(F32)
        work = cand
        taken = jnp.zeros(cand.shape, F32)
        for _ in range(PEER_TOPK):
            m = jnp.max(work, axis=0, keepdims=True)
            first = jnp.min(jnp.where(work == m, crow, float(PEER_TOPK * PEER_TOPK)), axis=0, keepdims=True)
            hit = crow == first
            taken = jnp.where(hit, 1.0, taken)
            work = jnp.where(hit, -jnp.inf, work)
        top = cand[0:1, :]
        zsum = jnp.sum(taken * jnp.exp(cand - top), axis=0, keepdims=True)
        c1 = jnp.zeros(s1.shape, F32)
        for a in range(PEER_TOPK):
            cnt = jnp.sum(taken[a * PEER_TOPK:(a + 1) * PEER_TOPK, :], axis=0, keepdims=True)
            c1 = jnp.where(rank1 == float(a), cnt, c1)
        c1_ref[h] = c1
        e1_ref[h] = jnp.exp(s1 - v1[0:1, :]) / zsum
        r2_ref[h] = rank2
        e2_ref[h] = jnp.exp(s2 - v2[0:1, :])


def _peer_route(h, g, wq, sk1, sk2, tn, row_off=0):
    t = h.shape[0] - row_off
    blk0 = row_off // tn
    full = lambda a: pl.BlockSpec(a.shape, lambda i: (0,) * a.ndim)
    tab = pl.BlockSpec((PEER_HEADS, PEER_NKEYS, tn), lambda i: (0, 0, i))
    tab_shape = jax.ShapeDtypeStruct((PEER_HEADS, PEER_NKEYS, t), F32)
    return pl.pallas_call(
        _peer_route_kernel,
        grid=(t // tn,),
        in_specs=[pl.BlockSpec((tn, D_MODEL), lambda i: (i + blk0, 0)), full(g), full(wq), full(sk1), full(sk2)],
        out_specs=[pl.BlockSpec((tn, D_MODEL), lambda i: (i, 0)), tab, tab, tab, tab],
        out_shape=[jax.ShapeDtypeStruct((t, D_MODEL), BF16), tab_shape, tab_shape, tab_shape, tab_shape],
        compiler_params=_cparams(("parallel",)),
        name="peer_route",
    )(h, g, wq, sk1, sk2)


def _peer_expert_kernel(h_ref, xn_ref, c1_ref, e1_ref, r2_ref, e2_ref, u_ref, v_ref, o_ref, acc_ref, *, n_i):
    e = pl.program_id(1)

    @pl.when(e == 0)
    def _():
        acc_ref[...] = jnp.zeros_like(acc_ref)

    nt = (((1,), (1,)), ((), ()))
    ht = lax.dot_general(u_ref[...], xn_ref[...], nt, preferred_element_type=F32)
    act = 0.5 * ht * (1.0 + lax.erf(ht * (2.0 ** -0.5)))
    rows = []
    for ii in range(n_i):
        gate = jnp.zeros((PEER_NKEYS, ht.shape[1]), F32)
        for h in range(PEER_HEADS):
            c1 = c1_ref[h, ii:ii + 1, :]
            e1 = e1_ref[h, ii:ii + 1, :]
            gate = gate + jnp.where(r2_ref[h] < c1, e2_ref[h] * e1, 0.0)
        rows.append(gate)
    gh = (jnp.concatenate(rows, axis=0) * act).astype(BF16)
    acc_ref[...] += lax.dot_general(gh, v_ref[...], (((0,), (0,)), ((), ())), preferred_element_type=F32)

    @pl.when(e == pl.num_programs(1) - 1)
    def _():
        o_ref[...] = h_ref[...] + acc_ref[...]


def _peer_experts(h, xn, c1, e1, r2, e2, u_bf, v_bf, tn, n_i, row_off=0):
    t = xn.shape[0]
    n_exp = u_bf.shape[0]
    eb = n_i * PEER_NKEYS
    blk0 = row_off // tn
    tab_i = pl.BlockSpec((PEER_HEADS, n_i, tn), lambda i, e: (0, e, i))
    tab_j = pl.BlockSpec((PEER_HEADS, PEER_NKEYS, tn), lambda i, e: (0, 0, i))
    return pl.pallas_call(
        functools.partial(_peer_expert_kernel, n_i=n_i),
        grid=(t // tn, n_exp // eb),
        in_specs=[pl.BlockSpec((tn, D_MODEL), lambda i, e: (i + blk0, 0)),
                  pl.BlockSpec((tn, D_MODEL), lambda i, e: (i, 0)),
                  tab_i, tab_i, tab_j, tab_j,
                  pl.BlockSpec((eb, D_MODEL), lambda i, e: (e, 0)),
                  pl.BlockSpec((eb, D_MODEL), lambda i, e: (e, 0))],
        out_specs=pl.BlockSpec((tn, D_MODEL), lambda i, e: (i, 0)),
        out_shape=jax.ShapeDtypeStruct((t, D_MODEL), F32),
        scratch_shapes=[pltpu.VMEM((tn, D_MODEL), F32)],
        compiler_params=_cparams(("parallel", "arbitrary")),
        name="peer_experts",
    )(h, xn, c1, e1, r2, e2, u_bf, v_bf)


PROMPT_PAD = 512 - N_META
ROW_TILE = 256
SCAN_CHUNK = 128
ATT_TQ = 256
ATT_TK = 256
PEER_TN = 512
PEER_NI = 8


def kernel(x_prompt, x_sample, cache_k, cache_v, page_table, state_shift, state_wkv, meta_tokens, norm_mix_g, w_in, shift_mu, decay_w0, decay_w2, iclr_a0, iclr_a2, gate_g2, k_k, k_a, r_k, gn_w, gn_b, q_norm_g, k_norm_g, sb_bias, w_out, norm_ffn_g, peer_wq, peer_subkeys1, peer_subkeys2, peer_u, peer_v):
    depth = w_in.shape[0]
    assert depth == 1 and x_prompt.shape[0] == 1
    l = 0
    seq = x_prompt.shape[1]
    db, s_new, _ = x_sample.shape
    bd = _seg_ones()
    lw = {'mu': shift_mu[l], 'w0': decay_w0[l], 'w2': decay_w2[l], 'a0': iclr_a0[l], 'a2': iclr_a2[l],
          'g2': gate_g2[l], 'k_k': k_k[l], 'k_a': k_a[l], 'r_k': r_k[l], 'gn_w': gn_w[l], 'gn_b': gn_b[l]}
    g_mix = norm_mix_g[l].reshape(1, D_MODEL)
    g_ffn = norm_ffn_g[l].reshape(1, D_MODEL)
    w_rw = w_in[l][:, :SHIFT_WIDTH].astype(BF16)
    w_sb = w_in[l][:, SHIFT_WIDTH:].astype(BF16)
    qg = jnp.tile(q_norm_g[l], N_HEADS).reshape(1, D_MODEL)
    kg = jnp.tile(k_norm_g[l], N_HEADS).reshape(1, D_MODEL)
    w_o = w_out[l].astype(BF16)
    u_bf = peer_u[l].astype(BF16)
    v_bf = peer_v[l].astype(BF16)

    pad = PROMPT_PAD
    t_all = pad + N_META + seq
    xp = jnp.concatenate([jnp.zeros((pad, D_MODEL), F32), meta_tokens.astype(F32), x_prompt[0]], axis=0)
    p_rw = _proj_rw(xp, g_mix, w_rw, ROW_TILE)
    qb, kn, kb, vv, vb, gs = _proj_sb(xp, g_mix, w_sb, qg, kg, bd, ROW_TILE)
    o_a, wkv_p = _rwkv(p_rw[None], jnp.zeros((1, 1, SHIFT_WIDTH), F32),
                       jnp.zeros((1, N_HEADS, HEAD_DIM, HEAD_DIM), F32), lw, bd, SCAN_CHUNK, SCAN_CHUNK)
    o_b = _sb_prompt(qb, kb, vb, sb_bias[l], ATT_TQ, ATT_TK, pad)
    h_p = _merge(xp, o_a[0], o_b, gs, w_o, ROW_TILE)
    xn_p, c1, e1, r2, e2 = _peer_route(h_p, g_ffn, peer_wq[l], peer_subkeys1[l], peer_subkeys2[l], ROW_TILE,
                                       row_off=pad + N_META)
    y_p = _peer_experts(h_p, xn_p, c1, e1, r2, e2, u_bf, v_bf, PEER_TN, PEER_NI, row_off=pad + N_META)

    k_prompt = kn[pad:].reshape(1, 1, N_META + seq, N_HEADS, HEAD_DIM)
    v_prompt = vv[pad:].reshape(1, 1, N_META + seq, N_HEADS, HEAD_DIM)
    shift_prompt = p_rw[t_all - 1].reshape(1, 1, SHIFT_WIDTH)

    n_s = db * s_new
    xs = x_sample.reshape(n_s, D_MODEL)
    ps_rw = _proj_rw(xs, g_mix, w_rw, n_s)
    qs, kns, _, vs, _, gss = _proj_sb(xs, g_mix, w_sb, qg, kg, bd, n_s)
    t_pad = 8
    ps_pad = jnp.pad(ps_rw.reshape(db, s_new, SHIFT_WIDTH), ((0, 0), (0, t_pad - s_new), (0, 0)))
    o_as, wkv_s = _rwkv(ps_pad, state_shift[l][:, None, :], state_wkv[l], lw, bd, t_pad, s_new)
    o_bs = _sb_sample(qs.astype(F32).reshape(db, s_new, D_MODEL), kns.reshape(db, s_new, D_MODEL),
                      vs.reshape(db, s_new, D_MODEL), cache_k[l], cache_v[l], page_table, sb_bias[l])
    h_s = _merge(xs, o_as[:, :s_new].reshape(n_s, D_MODEL), o_bs.reshape(n_s, D_MODEL), gss, w_o, n_s)
    xn_s, c1s, e1s, r2s, e2s = _peer_route(h_s, g_ffn, peer_wq[l], peer_subkeys1[l], peer_subkeys2[l], n_s)
    y_s = _peer_experts(h_s, xn_s, c1s, e1s, r2s, e2s, u_bf, v_bf, n_s, PEER_NI)

    return (y_p[None], y_s.reshape(db, s_new, D_MODEL), k_prompt, v_prompt, shift_prompt, wkv_p[None],
            kns.reshape(1, db, s_new, N_HEADS, HEAD_DIM), vs.reshape(1, db, s_new, N_HEADS, HEAD_DIM),
            ps_rw.reshape(db, s_new, SHIFT_WIDTH)[:, s_new - 1][None], wkv_s[None])
```

```python
import functools

import jax
import jax.numpy as jnp
from jax import lax
from jax.experimental import pallas as pl
from jax.experimental.pallas import tpu as pltpu

F32 = jnp.float32
BF16 = jnp.bfloat16
HIGHEST = lax.Precision.HIGHEST

D_MODEL = 1024
HEAD_DIM = 64
N_HEADS = 16
N_META = 16
RW_WIDTH = 1024
LORA_W = 64
GATE_W = 128
SHIFT_WIDTH = 3 * RW_WIDTH + 2 * LORA_W + GATE_W
NORM_EPS = 1e-6
GN_EPS = 64e-5
PEER_HEADS = 8
PEER_NKEYS = 128
PEER_HALF = 128
PEER_TOPK = 16
SEG_BLOCK = 256
VMEM_LIMIT = 56 * 1024 * 1024


def _cparams(sem):
    return pltpu.CompilerParams(dimension_semantics=sem, vmem_limit_bytes=VMEM_LIMIT)


def _seg_ones():
    r = jnp.arange(SEG_BLOCK) // HEAD_DIM
    return (r[:, None] == r[None, :]).astype(F32)


def _seg_sum(x, bd):
    parts = []
    for c in range(x.shape[1] // SEG_BLOCK):
        parts.append(jnp.dot(x[:, c * SEG_BLOCK:(c + 1) * SEG_BLOCK], bd, precision=HIGHEST,
                             preferred_element_type=F32))
    return jnp.concatenate(parts, axis=1)


def _split(x):
    hi = x.astype(BF16)
    return hi, (x - hi.astype(F32)).astype(BF16)


def _dot3(a, b, dims):
    dg = functools.partial(lax.dot_general, dimension_numbers=dims, preferred_element_type=F32)
    return dg(a[0], b[0]) + dg(a[0], b[1]) + dg(a[1], b[0])


def _rms_rows(x, g):
    return x * lax.rsqrt(jnp.mean(x * x, axis=-1, keepdims=True) + NORM_EPS) * g


def _softplus(x):
    sign_bit = jnp.uint32(0x80000000)
    neg_abs = lax.bitcast_convert_type(lax.bitcast_convert_type(x, jnp.uint32) | sign_bit, F32)
    return jnp.maximum(x, 0.0) + jnp.log(1.0 + jnp.exp(neg_abs))


def _sigmoid(x):
    return 1.0 / (1.0 + jnp.exp(-x))


def _proj_rw_kernel(x_ref, g_ref, w_ref, o_ref):
    xn = _rms_rows(x_ref[...], g_ref[...])
    o_ref[...] = jnp.dot(xn.astype(BF16), w_ref[...], preferred_element_type=F32)


def _proj_rw(x, g, w_bf, tm):
    t = x.shape[0]
    n = w_bf.shape[1]
    return pl.pallas_call(
        _proj_rw_kernel,
        grid=(t // tm,),
        in_specs=[pl.BlockSpec((tm, D_MODEL), lambda i: (i, 0)),
                  pl.BlockSpec((1, D_MODEL), lambda i: (0, 0)),
                  pl.BlockSpec((D_MODEL, n), lambda i: (0, 0))],
        out_specs=pl.BlockSpec((tm, n), lambda i: (i, 0)),
        out_shape=jax.ShapeDtypeStruct((t, n), F32),
        compiler_params=_cparams(("parallel",)),
        name="proj_rw",
    )(x, g, w_bf)


def _proj_sb_kernel(x_ref, g_ref, w_ref, qg_ref, kg_ref, bd_ref,
                    q_ref, k_ref, kb_ref, v_ref, vb_ref, gs_ref):
    xn = _rms_rows(x_ref[...], g_ref[...]).astype(BF16)
    p = jnp.dot(xn, w_ref[...], preferred_element_type=F32)
    bd = bd_ref[...]
    q = p[:, 0:D_MODEL]
    k = p[:, D_MODEL:2 * D_MODEL]
    v = p[:, 2 * D_MODEL:3 * D_MODEL]
    qn = q * lax.rsqrt(_seg_sum(q * q, bd) * (1.0 / HEAD_DIM) + NORM_EPS) * qg_ref[...]
    kn = k * lax.rsqrt(_seg_sum(k * k, bd) * (1.0 / HEAD_DIM) + NORM_EPS) * kg_ref[...]
    q_ref[...] = (qn * (HEAD_DIM ** -0.5)).astype(BF16)
    k_ref[...] = kn
    kb_ref[...] = kn.astype(BF16)
    v_ref[...] = v
    vb_ref[...] = v.astype(BF16)
    gs_ref[...] = _sigmoid(p[:, 3 * D_MODEL:5 * D_MODEL])


def _proj_sb(x, g, w_bf, qg, kg, bd, tm):
    t = x.shape[0]
    n = w_bf.shape[1]
    row = lambda w: pl.BlockSpec((tm, w), lambda i: (i, 0))
    full = lambda a: pl.BlockSpec(a.shape, lambda i: (0,) * a.ndim)
    return pl.pallas_call(
        _proj_sb_kernel,
        grid=(t // tm,),
        in_specs=[row(D_MODEL), full(g), full(w_bf), full(qg), full(kg), full(bd)],
        out_specs=[row(D_MODEL), row(D_MODEL), row(D_MODEL), row(D_MODEL), row(D_MODEL), row(2 * D_MODEL)],
        out_shape=[jax.ShapeDtypeStruct((t, D_MODEL), BF16), jax.ShapeDtypeStruct((t, D_MODEL), F32),
                   jax.ShapeDtypeStruct((t, D_MODEL), BF16), jax.ShapeDtypeStruct((t, D_MODEL), F32),
                   jax.ShapeDtypeStruct((t, D_MODEL), BF16), jax.ShapeDtypeStruct((t, 2 * D_MODEL), F32)],
        compiler_params=_cparams(("parallel",)),
        name="proj_sb",
    )(x, g, w_bf, qg, kg, bd)


def _rwkv_kernel(p_ref, sh0_ref, s0_ref, mu_ref, w0_ref, w2_ref, a0_ref, a2_ref, g2_ref, kk_ref, ka_ref,
                 rk_ref, gnw_ref, gnb_ref, bd_ref,
                 o_ref, st_out_ref,
                 st_ref, last_ref, y_s, *, n_steps):
    c = pl.program_id(1)
    tc = p_ref.shape[1]

    @pl.when(c == 0)
    def _():
        st_ref[...] = s0_ref[0]
        last_ref[...] = sh0_ref[0]

    p = p_ref[0]
    row = lax.broadcasted_iota(jnp.int32, p.shape, 0)
    prev = jnp.where(row == 0, last_ref[...], pltpu.roll(p, 1, axis=0))
    last_ref[...] = p[n_steps - 1:n_steps, :]
    xs = p + mu_ref[...] * (prev - p)
    r = xs[:, 0:RW_WIDTH]
    k = xs[:, RW_WIDTH:2 * RW_WIDTH]
    v = xs[:, 2 * RW_WIDTH:3 * RW_WIDTH]
    o1 = 3 * RW_WIDTH
    xw = xs[:, o1:o1 + LORA_W]
    xa = xs[:, o1 + LORA_W:o1 + 2 * LORA_W]
    xg = xs[:, o1 + 2 * LORA_W:o1 + 2 * LORA_W + GATE_W]
    bd = bd_ref[...]
    dot_hi = functools.partial(jnp.dot, precision=HIGHEST, preferred_element_type=F32)
    w = -_softplus(-(w0_ref[...] + dot_hi(jnp.tanh(xw), w2_ref[...]))) - 0.5
    logw = -jnp.exp(w)
    a = _sigmoid(a0_ref[...] + dot_hi(xa, a2_ref[...]))
    g = dot_hi(_sigmoid(xg), g2_ref[...])
    kk = k * kk_ref[...]
    kk = kk * lax.rsqrt(jnp.maximum(_seg_sum(kk * kk, bd), 1e-12))
    k = k * (1.0 + (a - 1.0) * ka_ref[...])
    b = kk * a
    k_in, v_in = k, v
    if n_steps < tc:
        live = (lax.broadcasted_iota(jnp.int32, (tc, 1), 0) < n_steps).astype(F32)
        logw, kk, b, k_in, v_in = logw * live, kk * live, b * live, k * live, v * live

    ti = lax.broadcasted_iota(jnp.int32, (tc, tc), 0)
    si = lax.broadcasted_iota(jnp.int32, (tc, tc), 1)
    incl = si <= ti
    strict = si < ti
    eye = (si == ti).astype(F32)
    cl = dot_hi(incl.astype(F32), logw)
    e_pos = jnp.exp(cl)
    e_neg = jnp.exp(-cl)
    at = -(kk * jnp.exp(cl - logw))
    bt = b * e_neg
    kt = k_in * e_neg
    rt = r * e_pos
    w_end = e_pos[tc - 1:tc, :]
    bh = bt * w_end
    kh = kt * w_end
    levels = []
    half = 1
    while half < tc:
        same = (ti >> half.bit_length()) == (si >> half.bit_length())
        levels.append(jnp.where(same, jnp.where((ti & half) != 0, jnp.where((si & half) == 0, 1.0, 0.0), 0.0), 0.0))
        half *= 2
    nt = (((1,), (1,)), ((), ()))
    nn = (((1,), (0,)), ((), ()))
    tn = (((0,), (0,)), ((), ()))
    t2 = lax.broadcasted_iota(jnp.int32, (tc, 2 * tc), 0)
    s2 = lax.broadcasted_iota(jnp.int32, (tc, 2 * tc), 1)
    right_strict = jnp.where(s2 >= tc, jnp.where(s2 - tc < t2, 1.0, 0.0), 0.0)
    both_incl = jnp.where((s2 & (tc - 1)) <= t2, 1.0, 0.0)
    heads = range(N_HEADS)
    sls = [slice(h * HEAD_DIM, (h + 1) * HEAD_DIM) for h in heads]
    s0 = [st_ref[h] for h in heads]
    x2 = [_split(jnp.concatenate([at[:, sl], rt[:, sl]], axis=0)) for sl in sls]
    p = [_dot3(x2[h], _split(jnp.concatenate([bt[:, sls[h]], kt[:, sls[h]]], axis=0)), nt)
         for h in heads]
    n = [jnp.where(strict, p[h][0:tc, 0:tc], 0.0) for h in heads]
    t_inv = [eye + n[h] * levels[0] for h in heads]
    for lv in levels[1:]:
        ts = [_split(t) for t in t_inv]
        x = [_split(_dot3(_split(n[h] * lv), ts[h], nn)) for h in heads]
        t_inv = [t_inv[h] + _dot3(ts[h], x[h], nn) for h in heads]
    ps = [_dot3(x2[h], _split(s0[h]), nt) for h in heads]
    vh = [v_in[:, sl] for sl in sls]
    rhs = [ps[h][0:tc] + _dot3(_split(p[h][0:tc] * right_strict), _split(jnp.concatenate([vh[h], vh[h]], axis=0)), nn)
           for h in heads]
    u = [_dot3(_split(t_inv[h]), _split(rhs[h]), nn) for h in heads]
    uv = [_split(jnp.concatenate([u[h], vh[h]], axis=0)) for h in heads]
    for h in heads:
        y_s[:, sls[h]] = ps[h][tc:2 * tc] + _dot3(_split(p[h][tc:2 * tc] * both_incl), uv[h], nn)
        st_ref[h] = (s0[h] * w_end[:, sls[h]]
                     + _dot3(uv[h], _split(jnp.concatenate([bh[:, sls[h]], kh[:, sls[h]]], axis=0)), tn))

    y = y_s[...]
    mean = _seg_sum(y, bd) * (1.0 / HEAD_DIM)
    yc = y - mean
    var = _seg_sum(yc * yc, bd) * (1.0 / HEAD_DIM)
    y = yc * lax.rsqrt(var + GN_EPS) * gnw_ref[...] + gnb_ref[...]
    y = y + _seg_sum(r * k * rk_ref[...], bd) * v
    o_ref[0] = y * g

    @pl.when(c == pl.num_programs(1) - 1)
    def _():
        st_out_ref[0] = st_ref[...]


def _rwkv(p_rw, shift0, s0, lw, bd, tc, n_steps):
    b, t, _ = p_rw.shape
    nc = t // tc
    assert n_steps == tc or nc == 1
    vec = lambda a: a.reshape(1, -1).astype(F32)
    params = [vec(lw['mu']), vec(lw['w0']), lw['w2'], vec(lw['a0']), lw['a2'], lw['g2'], vec(lw['k_k']),
              vec(lw['k_a']), vec(lw['r_k']), vec(lw['gn_w']), vec(lw['gn_b']), bd]
    full = lambda a: pl.BlockSpec(a.shape, lambda i, c: (0,) * a.ndim)
    return pl.pallas_call(
        functools.partial(_rwkv_kernel, n_steps=n_steps),
        grid=(b, nc),
        in_specs=[pl.BlockSpec((1, tc, SHIFT_WIDTH), lambda i, c: (i, c, 0)),
                  pl.BlockSpec((1, 1, SHIFT_WIDTH), lambda i, c: (i, 0, 0)),
                  pl.BlockSpec((1, N_HEADS, HEAD_DIM, HEAD_DIM), lambda i, c: (i, 0, 0, 0))]
                 + [full(a) for a in params],
        out_specs=[pl.BlockSpec((1, tc, RW_WIDTH), lambda i, c: (i, c, 0)),
                   pl.BlockSpec((1, N_HEADS, HEAD_DIM, HEAD_DIM), lambda i, c: (i, 0, 0, 0))],
        out_shape=[jax.ShapeDtypeStruct((b, t, RW_WIDTH), F32),
                   jax.ShapeDtypeStruct((b, N_HEADS, HEAD_DIM, HEAD_DIM), F32)],
        scratch_shapes=[pltpu.VMEM((N_HEADS, HEAD_DIM, HEAD_DIM), F32),
                        pltpu.VMEM((1, SHIFT_WIDTH), F32), pltpu.VMEM((tc, RW_WIDTH), F32)],
        compiler_params=_cparams(("parallel", "arbitrary")),
        name="rwkv7",
    )(p_rw, shift0, s0, *params)


def _sb_prompt_kernel(q_ref, k_ref, v_ref, b_ref, tri_ref, o_ref, acc_ref, *, tq, tk, pad):
    i = pl.program_id(1)
    q = q_ref[...]
    lane = lax.broadcasted_iota(jnp.int32, (1, 2 * HEAD_DIM), 1)
    halves = (lane < HEAD_DIM, lane >= HEAD_DIM)
    qh = [jnp.where(m, q, jnp.zeros_like(q)) for m in halves]
    bias = [b_ref[h][:, 0:1] for h in range(2)]
    tri = tri_ref[...]
    causal = (lax.broadcasted_iota(jnp.int32, (tq, tk), 1) < lax.broadcasted_iota(jnp.int32, (tq, tk), 0))
    nt = (((1,), (1,)), ((), ()))
    acc_ref[...] = jnp.zeros_like(acc_ref)

    def tiles(js, carry, diagonal):
        ch = [(t, h) for t in range(len(js)) for h in range(2)]
        offs = [pl.multiple_of(j * tk, tk) for j in js]
        ks = [k_ref[pl.ds(o, tk), :] for o in offs]
        vs = [v_ref[pl.ds(o, tk), :] for o in offs]
        z = [lax.dot_general(qh[h], ks[t], nt, preferred_element_type=F32) + bias[h] for t, h in ch]
        sp = [_softplus(x) for x in z]
        spm = [jnp.where(causal, x, 0.0) for x in sp] if diagonal else sp
        rows = [jnp.sum(x, axis=1, keepdims=True) for x in spm]
        cs = []
        cur = list(carry)
        for t, h in ch:
            cs.append(cur[h])
            cur[h] = cur[h] + rows[2 * t + h]
        after = [jnp.dot(spm[c].astype(BF16), tri, preferred_element_type=F32) + cs[c] for c in range(len(ch))]
        a = [jnp.exp((z[c] - sp[c]) - after[c]) for c in range(len(ch))]
        if diagonal:
            a = [jnp.where(causal, x, 0.0) for x in a]
        out = None
        for c, (t, h) in enumerate(ch):
            vh = jnp.where(halves[h], vs[t], jnp.zeros_like(vs[t]))
            d = jnp.dot(a[c].astype(BF16), vh, preferred_element_type=F32)
            out = d if out is None else out + d
        acc_ref[...] += out
        return tuple(cur)

    zero = jnp.zeros((tq, 1), F32)
    carry = tiles([i], (zero, zero), True)
    n_rest = jnp.maximum(i - pad // tk, 0)
    carry = lax.fori_loop(0, n_rest >> 1, lambda m, c: tiles([i - 1 - 2 * m, i - 2 - 2 * m], c, False), carry)

    @pl.when((n_rest & 1) == 1)
    def _():
        tiles([pad // tk], carry, False)

    o_ref[...] = acc_ref[...]


def _sb_prompt(qb, kb, vb, bias, tq, tk, pad):
    t = qb.shape[0]
    assert tq == tk, "the causal mask is applied on the diagonal tile only"
    n_pairs = N_HEADS // 2
    tri = (jnp.arange(tk)[:, None] > jnp.arange(tk)[None, :]).astype(BF16)
    bias_l = jnp.broadcast_to(bias.astype(F32).reshape(N_HEADS, 1, 1), (N_HEADS, 1, 2 * HEAD_DIM))
    return pl.pallas_call(
        functools.partial(_sb_prompt_kernel, tq=tq, tk=tk, pad=pad),
        grid=(n_pairs, t // tq),
        in_specs=[pl.BlockSpec((tq, 2 * HEAD_DIM), lambda p, i: (i, p)),
                  pl.BlockSpec((t, 2 * HEAD_DIM), lambda p, i: (0, p)),
                  pl.BlockSpec((t, 2 * HEAD_DIM), lambda p, i: (0, p)),
                  pl.BlockSpec((2, 1, 2 * HEAD_DIM), lambda p, i: (p, 0, 0)),
                  pl.BlockSpec((tk, tk), lambda p, i: (0, 0))],
        out_specs=pl.BlockSpec((tq, 2 * HEAD_DIM), lambda p, i: (i, p)),
        out_shape=jax.ShapeDtypeStruct((t, D_MODEL), F32),
        scratch_shapes=[pltpu.VMEM((tq, 2 * HEAD_DIM), F32)],
        compiler_params=_cparams(("parallel", "parallel")),
        name="sb_prompt",
    )(qb, kb, vb, bias_l, tri)


def _sb_sample_kernel(pt_ref, q_ref, kn_ref, vn_ref, *rest, n_slots, s_new):
    del pt_ref
    kt_refs, vt_refs = rest[0:n_slots], rest[n_slots:2 * n_slots]
    b_ref, tri_ref, dmask_ref, o_ref, acc_ref, carry_ref = rest[2 * n_slots:]
    step = pl.program_id(1)
    q = q_ref[0]
    bias = b_ref[...]
    tri = tri_ref[...]
    nt = (((1,), (1,)), ((), ()))

    def weights(z, mask):
        sp = [_softplus(x) for x in z]
        spm = sp if mask is None else [jnp.where(mask, x, 0.0) for x in sp]
        carry = carry_ref[...]
        a = []
        for t in range(len(z)):
            after = jnp.dot(spm[t], tri, precision=HIGHEST, preferred_element_type=F32) + carry
            x = jnp.exp((z[t] - sp[t]) - after)
            a.append((x if mask is None else jnp.where(mask, x, 0.0)).astype(BF16))
            carry = carry + jnp.sum(spm[t], axis=1, keepdims=True)
        carry_ref[...] = carry
        return a

    @pl.when(step == 0)
    def _():
        acc_ref[...] = jnp.zeros_like(acc_ref)
        carry_ref[...] = jnp.zeros_like(carry_ref)
        z = lax.dot_general(q, kn_ref[0].astype(BF16), nt, preferred_element_type=F32) + bias
        kidx = lax.broadcasted_iota(jnp.int32, z.shape, 1)
        qidx = lax.broadcasted_iota(jnp.int32, z.shape, 0) // N_HEADS
        a, = weights([z], kidx < qidx)
        acc_ref[...] += jnp.dot(a, vn_ref[0].astype(BF16), preferred_element_type=F32)

    z = [jnp.dot(q, kt_refs[t][0].astype(BF16), preferred_element_type=F32) + bias for t in range(n_slots)]
    a = weights(z, None)
    out = None
    for t in range(n_slots):
        d = lax.dot_general(a[t], vt_refs[t][0].astype(BF16), nt, preferred_element_type=F32)
        out = d if out is None else out + d
    acc_ref[...] += out

    @pl.when(step == pl.num_programs(1) - 1)
    def _():
        sel = acc_ref[...] * dmask_ref[...]
        for qi in range(s_new):
            o_ref[0, qi:qi + 1, :] = jnp.sum(sel[qi * N_HEADS:(qi + 1) * N_HEADS, :], axis=0, keepdims=True)


def _sb_sample(qn, kn, v, cache_k, cache_v, page_table, bias):
    db, s_new, _ = qn.shape
    n_phys, psize = cache_k.shape[0], cache_k.shape[1]
    n_pages = page_table.shape[1]
    n_slots = SAMPLE_PAGES_PER_STEP
    assert n_pages % n_slots == 0
    nrow = s_new * N_HEADS
    kt = cache_k.transpose(0, 2, 3, 1).reshape(n_phys, D_MODEL, psize)
    vt = cache_v.transpose(0, 2, 3, 1).reshape(n_phys, D_MODEL, psize)
    head_of_lane = jnp.arange(D_MODEL) // HEAD_DIM
    row_head = jnp.arange(nrow) % N_HEADS
    dmask = (row_head[:, None] == head_of_lane[None, :]).astype(F32)
    qbd = (jnp.repeat(qn, N_HEADS, axis=1) * dmask[None]).astype(BF16)
    padn = lambda a: jnp.pad(a, ((0, 0), (0, psize - s_new), (0, 0)))
    bias_r = bias.astype(F32)[row_head].reshape(nrow, 1)
    tri = (jnp.arange(psize)[:, None] > jnp.arange(psize)[None, :]).astype(F32)
    const = lambda a: pl.BlockSpec(a.shape, lambda b, j, pt: (0, 0))
    page = lambda s: pl.BlockSpec((1, D_MODEL, psize),
                                  lambda b, j, pt: (pt[b, n_pages - 1 - (j * n_slots + s)], 0, 0))
    grid_spec = pltpu.PrefetchScalarGridSpec(
        num_scalar_prefetch=1,
        grid=(db, n_pages // n_slots),
        in_specs=[pl.BlockSpec((1, nrow, D_MODEL), lambda b, j, pt: (b, 0, 0)),
                  pl.BlockSpec((1, psize, D_MODEL), lambda b, j, pt: (b, 0, 0)),
                  pl.BlockSpec((1, psize, D_MODEL), lambda b, j, pt: (b, 0, 0))]
                 + [page(s) for s in range(n_slots)] * 2
                 + [const(bias_r), const(tri), const(dmask)],
        out_specs=pl.BlockSpec((1, s_new, D_MODEL), lambda b, j, pt: (b, 0, 0)),
        scratch_shapes=[pltpu.VMEM((nrow, D_MODEL), F32), pltpu.VMEM((nrow, 1), F32)],
    )
    return pl.pallas_call(
        functools.partial(_sb_sample_kernel, n_slots=n_slots, s_new=s_new),
        grid_spec=grid_spec,
        out_shape=jax.ShapeDtypeStruct((db, s_new, D_MODEL), F32),
        compiler_params=_cparams(("parallel", "arbitrary")),
        name="sb_sample",
    )(page_table, qbd, padn(kn), padn(v), *([kt] * n_slots), *([vt] * n_slots), bias_r, tri, dmask)


def _merge_kernel(x_ref, oa_ref, ob_ref, gs_ref, w_ref, o_ref):
    gs = gs_ref[...]
    merged = gs[:, 0:D_MODEL] * oa_ref[...] + gs[:, D_MODEL:2 * D_MODEL] * ob_ref[...]
    o_ref[...] = x_ref[...] + jnp.dot(merged.astype(BF16), w_ref[...], preferred_element_type=F32)


def _merge(x, oa, ob, gs, w_bf, tm):
    t = x.shape[0]
    row = lambda w: pl.BlockSpec((tm, w), lambda i: (i, 0))
    return pl.pallas_call(
        _merge_kernel,
        grid=(t // tm,),
        in_specs=[row(D_MODEL), row(D_MODEL), row(D_MODEL), row(2 * D_MODEL),
                  pl.BlockSpec((D_MODEL, D_MODEL), lambda i: (0, 0))],
        out_specs=row(D_MODEL),
        out_shape=jax.ShapeDtypeStruct((t, D_MODEL), F32),
        compiler_params=_cparams(("parallel",)),
        name="merge_out",
    )(x, oa, ob, gs, w_bf)


def _top16(s):
    n = s.shape[0]
    row = lax.broadcasted_iota(jnp.int32, s.shape, 0).astype(F32)
    rank = jnp.full(s.shape, float(PEER_TOPK), F32)
    vals = []
    for it in range(PEER_TOPK):
        m = jnp.max(s, axis=0, keepdims=True)
        first = jnp.min(jnp.where(s == m, row, float(n)), axis=0, keepdims=True)
        hit = row == first
        rank = jnp.where(hit, float(it), rank)
        s = jnp.where(hit, -jnp.inf, s)
        vals.append(m)
    return jnp.concatenate(vals, axis=0), rank


def _peer_route_kernel(h_ref, g_ref, wq_ref, sk1_ref, sk2_ref,
                       xn_ref, c1_ref, e1_ref, r2_ref, e2_ref):
    xn = _rms_rows(h_ref[...], g_ref[...])
    xn_ref[...] = xn.astype(BF16)
    q = jnp.dot(xn, wq_ref[...], precision=HIGHEST, preferred_element_type=F32)
    nt = (((1,), (1,)), ((), ()))
    for h in range(PEER_HEADS):
        q1 = q[:, h * 2 * PEER_HALF:h * 2 * PEER_HALF + PEER_HALF]
        q2 = q[:, h * 2 * PEER_HALF + PEER_HALF:(h + 1) * 2 * PEER_HALF]
        s1 = lax.dot_general(sk1_ref[h], q1, nt, precision=HIGHEST, preferred_element_type=F32)
        s2 = lax.dot_general(sk2_ref[h], q2, nt, precision=HIGHEST, preferred_element_type=F32)
        v1, rank1 = _top16(s1)
        v2, rank2 = _top16(s2)
        cand = jnp.concatenate([v1[a:a + 1, :] + v2 for a in range(PEER_TOPK)], axis=0)
        crow = lax.broadcasted_iota(jnp.int32, cand.shape, 0).astype(F32)
        work = cand
        taken = jnp.zeros(cand.shape, F32)
        for _ in range(PEER_TOPK):
            m = jnp.max(work, axis=0, keepdims=True)
            first = jnp.min(jnp.where(work == m, crow, float(PEER_TOPK * PEER_TOPK)), axis=0, keepdims=True)
            hit = crow == first
            taken = jnp.where(hit, 1.0, taken)
            work = jnp.where(hit, -jnp.inf, work)
        top = cand[0:1, :]
        zsum = jnp.sum(taken * jnp.exp(cand - top), axis=0, keepdims=True)
        c1 = jnp.zeros(s1.shape, F32)
        for a in range(PEER_TOPK):
            cnt = jnp.sum(taken[a * PEER_TOPK:(a + 1) * PEER_TOPK, :], axis=0, keepdims=True)
            c1 = jnp.where(rank1 == float(a), cnt, c1)
        c1_ref[h] = c1
        e1_ref[h] = jnp.exp(s1 - v1[0:1, :]) / zsum
        r2_ref[h] = rank2
        e2_ref[h] = jnp.exp(s2 - v2[0:1, :])


def _peer_route(h, g, wq, sk1, sk2, tn, row_off=0):
    t = h.shape[0] - row_off
    blk0 = row_off // tn
    full = lambda a: pl.BlockSpec(a.shape, lambda i: (0,) * a.ndim)
    tab = pl.BlockSpec((PEER_HEADS, PEER_NKEYS, tn), lambda i: (0, 0, i))
    tab_shape = jax.ShapeDtypeStruct((PEER_HEADS, PEER_NKEYS, t), F32)
    return pl.pallas_call(
        _peer_route_kernel,
        grid=(t // tn,),
        in_specs=[pl.BlockSpec((tn, D_MODEL), lambda i: (i + blk0, 0)), full(g), full(wq), full(sk1), full(sk2)],
        out_specs=[pl.BlockSpec((tn, D_MODEL), lambda i: (i, 0)), tab, tab, tab, tab],
        out_shape=[jax.ShapeDtypeStruct((t, D_MODEL), BF16), tab_shape, tab_shape, tab_shape, tab_shape],
        compiler_params=_cparams(("parallel",)),
        name="peer_route",
    )(h, g, wq, sk1, sk2)


def _peer_expert_kernel(h_ref, xn_ref, c1_ref, e1_ref, r2_ref, e2_ref, u_ref, v_ref, o_ref, acc_ref, *, n_i):
    e = pl.program_id(1)

    @pl.when(e == 0)
    def _():
        acc_ref[...] = jnp.zeros_like(acc_ref)

    nt = (((1,), (1,)), ((), ()))
    ht = lax.dot_general(u_ref[...], xn_ref[...], nt, preferred_element_type=F32)
    act = 0.5 * ht * (1.0 + lax.erf(ht * (2.0 ** -0.5)))
    rows = []
    for ii in range(n_i):
        gate = jnp.zeros((PEER_NKEYS, ht.shape[1]), F32)
        for h in range(PEER_HEADS):
            c1 = c1_ref[h, ii:ii + 1, :]
            e1 = e1_ref[h, ii:ii + 1, :]
            gate = gate + jnp.where(r2_ref[h] < c1, e2_ref[h] * e1, 0.0)
        rows.append(gate)
    gh = (jnp.concatenate(rows, axis=0) * act).astype(BF16)
    acc_ref[...] += lax.dot_general(gh, v_ref[...], (((0,), (0,)), ((), ())), preferred_element_type=F32)

    @pl.when(e == pl.num_programs(1) - 1)
    def _():
        o_ref[...] = h_ref[...] + acc_ref[...]


def _peer_experts(h, xn, c1, e1, r2, e2, u_bf, v_bf, tn, n_i, row_off=0):
    t = xn.shape[0]
    n_exp = u_bf.shape[0]
    eb = n_i * PEER_NKEYS
    blk0 = row_off // tn
    tab_i = pl.BlockSpec((PEER_HEADS, n_i, tn), lambda i, e: (0, e, i))
    tab_j = pl.BlockSpec((PEER_HEADS, PEER_NKEYS, tn), lambda i, e: (0, 0, i))
    return pl.pallas_call(
        functools.partial(_peer_expert_kernel, n_i=n_i),
        grid=(t // tn, n_exp // eb),
        in_specs=[pl.BlockSpec((tn, D_MODEL), lambda i, e: (i + blk0, 0)),
                  pl.BlockSpec((tn, D_MODEL), lambda i, e: (i, 0)),
                  tab_i, tab_i, tab_j, tab_j,
                  pl.BlockSpec((eb, D_MODEL), lambda i, e: (e, 0)),
                  pl.BlockSpec((eb, D_MODEL), lambda i, e: (e, 0))],
        out_specs=pl.BlockSpec((tn, D_MODEL), lambda i, e: (i, 0)),
        out_shape=jax.ShapeDtypeStruct((t, D_MODEL), F32),
        scratch_shapes=[pltpu.VMEM((tn, D_MODEL), F32)],
        compiler_params=_cparams(("parallel", "arbitrary")),
        name="peer_experts",
    )(h, xn, c1, e1, r2, e2, u_bf, v_bf)


PROMPT_PAD = 512 - N_META
ROW_TILE = 256
SCAN_CHUNK = 64
ATT_TQ = 256
ATT_TK = 256
PEER_TN = 512
PEER_NI = 8
SAMPLE_PAGES_PER_STEP = 4


def kernel(x_prompt, x_sample, cache_k, cache_v, page_table, state_shift, state_wkv, meta_tokens, norm_mix_g, w_in, shift_mu, decay_w0, decay_w2, iclr_a0, iclr_a2, gate_g2, k_k, k_a, r_k, gn_w, gn_b, q_norm_g, k_norm_g, sb_bias, w_out, norm_ffn_g, peer_wq, peer_subkeys1, peer_subkeys2, peer_u, peer_v):
    depth = w_in.shape[0]
    assert depth == 1 and x_prompt.shape[0] == 1
    l = 0
    seq = x_prompt.shape[1]
    db, s_new, _ = x_sample.shape
    bd = _seg_ones()
    lw = {'mu': shift_mu[l], 'w0': decay_w0[l], 'w2': decay_w2[l], 'a0': iclr_a0[l], 'a2': iclr_a2[l],
          'g2': gate_g2[l], 'k_k': k_k[l], 'k_a': k_a[l], 'r_k': r_k[l], 'gn_w': gn_w[l], 'gn_b': gn_b[l]}
    g_mix = norm_mix_g[l].reshape(1, D_MODEL)
    g_ffn = norm_ffn_g[l].reshape(1, D_MODEL)
    w_rw = w_in[l][:, :SHIFT_WIDTH].astype(BF16)
    w_sb = w_in[l][:, SHIFT_WIDTH:].astype(BF16)
    qg = jnp.tile(q_norm_g[l], N_HEADS).reshape(1, D_MODEL)
    kg = jnp.tile(k_norm_g[l], N_HEADS).reshape(1, D_MODEL)
    w_o = w_out[l].astype(BF16)
    u_bf = peer_u[l].astype(BF16)
    v_bf = peer_v[l].astype(BF16)

    pad = PROMPT_PAD
    t_all = pad + N_META + seq
    xp = jnp.concatenate([jnp.zeros((pad, D_MODEL), F32), meta_tokens.astype(F32), x_prompt[0]], axis=0)
    p_rw = _proj_rw(xp, g_mix, w_rw, ROW_TILE)
    qb, kn, kb, vv, vb, gs = _proj_sb(xp, g_mix, w_sb, qg, kg, bd, ROW_TILE)
    o_a, wkv_p = _rwkv(p_rw[None], jnp.zeros((1, 1, SHIFT_WIDTH), F32),
                       jnp.zeros((1, N_HEADS, HEAD_DIM, HEAD_DIM), F32), lw, bd, SCAN_CHUNK, SCAN_CHUNK)
    o_b = _sb_prompt(qb, kb, vb, sb_bias[l], ATT_TQ, ATT_TK, pad)
    h_p = _merge(xp, o_a[0], o_b, gs, w_o, ROW_TILE)
    xn_p, c1, e1, r2, e2 = _peer_route(h_p, g_ffn, peer_wq[l], peer_subkeys1[l], peer_subkeys2[l], ROW_TILE,
                                       row_off=pad + N_META)
    y_p = _peer_experts(h_p, xn_p, c1, e1, r2, e2, u_bf, v_bf, PEER_TN, PEER_NI, row_off=pad + N_META)

    k_prompt = kn[pad:].reshape(1, 1, N_META + seq, N_HEADS, HEAD_DIM)
    v_prompt = vv[pad:].reshape(1, 1, N_META + seq, N_HEADS, HEAD_DIM)
    shift_prompt = p_rw[t_all - 1].reshape(1, 1, SHIFT_WIDTH)

    n_s = db * s_new
    xs = x_sample.reshape(n_s, D_MODEL)
    ps_rw = _proj_rw(xs, g_mix, w_rw, n_s)
    qs, kns, _, vs, _, gss = _proj_sb(xs, g_mix, w_sb, qg, kg, bd, n_s)
    ps_pad = jnp.pad(ps_rw.reshape(db, s_new, SHIFT_WIDTH), ((0, 0), (0, SCAN_CHUNK - s_new), (0, 0)))
    o_as, wkv_s = _rwkv(ps_pad, state_shift[l][:, None, :], state_wkv[l], lw, bd, SCAN_CHUNK, s_new)
    o_bs = _sb_sample(qs.astype(F32).reshape(db, s_new, D_MODEL), kns.reshape(db, s_new, D_MODEL),
                      vs.reshape(db, s_new, D_MODEL), cache_k[l], cache_v[l], page_table, sb_bias[l])
    h_s = _merge(xs, o_as[:, :s_new].reshape(n_s, D_MODEL), o_bs.reshape(n_s, D_MODEL), gss, w_o, n_s)
    xn_s, c1s, e1s, r2s, e2s = _peer_route(h_s, g_ffn, peer_wq[l], peer_subkeys1[l], peer_subkeys2[l], n_s)
    y_s = _peer_experts(h_s, xn_s, c1s, e1s, r2s, e2s, u_bf, v_bf, n_s, PEER_NI)

    return (y_p[None], y_s.reshape(db, s_new, D_MODEL), k_prompt, v_prompt, shift_prompt, wkv_p[None],
            kns.reshape(1, db, s_new, N_HEADS, HEAD_DIM), vs.reshape(1, db, s_new, N_HEADS, HEAD_DIM),
            ps_rw.reshape(db, s_new, SHIFT_WIDTH)[:, s_new - 1][None], wkv_s[None])
```

```python
import functools

import jax
import jax.numpy as jnp
from jax import lax
from jax.experimental import pallas as pl
from jax.experimental.pallas import tpu as pltpu

F32 = jnp.float32
BF16 = jnp.bfloat16
HIGHEST = lax.Precision.HIGHEST

D_MODEL = 1024
HEAD_DIM = 64
N_HEADS = 16
N_META = 16
RW_WIDTH = 1024
LORA_W = 64
GATE_W = 128
SHIFT_WIDTH = 3 * RW_WIDTH + 2 * LORA_W + GATE_W
NORM_EPS = 1e-6
LOG2E = 1.4426950408889634
GN_EPS = 64e-5
PEER_HEADS = 8
PEER_NKEYS = 128
PEER_HALF = 128
PEER_TOPK = 16
SEG_BLOCK = 256
VMEM_LIMIT = 56 * 1024 * 1024


def _cparams(sem):
    return pltpu.CompilerParams(dimension_semantics=sem, vmem_limit_bytes=VMEM_LIMIT)


def _seg_ones():
    r = jnp.arange(SEG_BLOCK) // HEAD_DIM
    return (r[:, None] == r[None, :]).astype(F32)


def _seg_sum(x, bd):
    parts = []
    for c in range(x.shape[1] // SEG_BLOCK):
        parts.append(jnp.dot(x[:, c * SEG_BLOCK:(c + 1) * SEG_BLOCK], bd, precision=HIGHEST,
                             preferred_element_type=F32))
    return jnp.concatenate(parts, axis=1)


def _split(x):
    hi = x.astype(BF16)
    return hi, (x - hi.astype(F32)).astype(BF16)


def _dot3(a, b, dims):
    dg = functools.partial(lax.dot_general, dimension_numbers=dims, preferred_element_type=F32)
    return dg(a[0], b[0]) + dg(a[0], b[1]) + dg(a[1], b[0])


def _rms_rows(x, g):
    return x * lax.rsqrt(jnp.mean(x * x, axis=-1, keepdims=True) + NORM_EPS) * g


def _softplus(x):
    sign_bit = jnp.uint32(0x80000000)
    neg_abs = lax.bitcast_convert_type(lax.bitcast_convert_type(x, jnp.uint32) | sign_bit, F32)
    return jnp.maximum(x, 0.0) + jnp.log(1.0 + jnp.exp(neg_abs))


def _softplus2(x):
    sign_bit = jnp.uint32(0x80000000)
    neg_abs = lax.bitcast_convert_type(lax.bitcast_convert_type(x, jnp.uint32) | sign_bit, F32)
    return jnp.maximum(x, 0.0) + jnp.log(1.0 + jnp.exp2(neg_abs)) * LOG2E


def _sigmoid(x):
    return 1.0 / (1.0 + jnp.exp(-x))


def _proj_rw_kernel(x_ref, g_ref, w_ref, o_ref):
    xn = _rms_rows(x_ref[...], g_ref[...])
    o_ref[...] = jnp.dot(xn.astype(BF16), w_ref[...], preferred_element_type=F32)


def _proj_rw(x, g, w_bf, tm):
    t = x.shape[0]
    n = w_bf.shape[1]
    return pl.pallas_call(
        _proj_rw_kernel,
        grid=(t // tm,),
        in_specs=[pl.BlockSpec((tm, D_MODEL), lambda i: (i, 0)),
                  pl.BlockSpec((1, D_MODEL), lambda i: (0, 0)),
                  pl.BlockSpec((D_MODEL, n), lambda i: (0, 0))],
        out_specs=pl.BlockSpec((tm, n), lambda i: (i, 0)),
        out_shape=jax.ShapeDtypeStruct((t, n), F32),
        compiler_params=_cparams(("parallel",)),
        name="proj_rw",
    )(x, g, w_bf)


def _proj_sb_kernel(x_ref, g_ref, w_ref, qg_ref, kg_ref, bd_ref,
                    q_ref, k_ref, kb_ref, v_ref, vb_ref, gs_ref, *, q_scale):
    xn = _rms_rows(x_ref[...], g_ref[...]).astype(BF16)
    p = jnp.dot(xn, w_ref[...], preferred_element_type=F32)
    bd = bd_ref[...]
    q = p[:, 0:D_MODEL]
    k = p[:, D_MODEL:2 * D_MODEL]
    v = p[:, 2 * D_MODEL:3 * D_MODEL]
    qn = q * lax.rsqrt(_seg_sum(q * q, bd) * (1.0 / HEAD_DIM) + NORM_EPS) * qg_ref[...]
    kn = k * lax.rsqrt(_seg_sum(k * k, bd) * (1.0 / HEAD_DIM) + NORM_EPS) * kg_ref[...]
    q_ref[...] = (qn * q_scale).astype(BF16)
    k_ref[...] = kn
    kb_ref[...] = kn.astype(BF16)
    v_ref[...] = v
    vb_ref[...] = v.astype(BF16)
    gs_ref[...] = _sigmoid(p[:, 3 * D_MODEL:5 * D_MODEL])


def _proj_sb(x, g, w_bf, qg, kg, bd, tm, q_scale):
    t = x.shape[0]
    n = w_bf.shape[1]
    row = lambda w: pl.BlockSpec((tm, w), lambda i: (i, 0))
    full = lambda a: pl.BlockSpec(a.shape, lambda i: (0,) * a.ndim)
    return pl.pallas_call(
        functools.partial(_proj_sb_kernel, q_scale=q_scale),
        grid=(t // tm,),
        in_specs=[row(D_MODEL), full(g), full(w_bf), full(qg), full(kg), full(bd)],
        out_specs=[row(D_MODEL), row(D_MODEL), row(D_MODEL), row(D_MODEL), row(D_MODEL), row(2 * D_MODEL)],
        out_shape=[jax.ShapeDtypeStruct((t, D_MODEL), BF16), jax.ShapeDtypeStruct((t, D_MODEL), F32),
                   jax.ShapeDtypeStruct((t, D_MODEL), BF16), jax.ShapeDtypeStruct((t, D_MODEL), F32),
                   jax.ShapeDtypeStruct((t, D_MODEL), BF16), jax.ShapeDtypeStruct((t, 2 * D_MODEL), F32)],
        compiler_params=_cparams(("parallel",)),
        name="proj_sb",
    )(x, g, w_bf, qg, kg, bd)


def _rwkv_kernel(p_ref, sh0_ref, s0_ref, mu_ref, w0_ref, w2_ref, a0_ref, a2_ref, g2_ref, kk_ref, ka_ref,
                 rk_ref, gnw_ref, gnb_ref, bd_ref,
                 o_ref, st_out_ref,
                 st_ref, last_ref, y_s, *, n_steps):
    c = pl.program_id(1)
    tc = p_ref.shape[1]

    @pl.when(c == 0)
    def _():
        st_ref[...] = s0_ref[0]
        last_ref[...] = sh0_ref[0]

    p = p_ref[0]
    row = lax.broadcasted_iota(jnp.int32, p.shape, 0)
    prev = jnp.where(row == 0, last_ref[...], pltpu.roll(p, 1, axis=0))
    last_ref[...] = p[n_steps - 1:n_steps, :]
    xs = p + mu_ref[...] * (prev - p)
    r = xs[:, 0:RW_WIDTH]
    k = xs[:, RW_WIDTH:2 * RW_WIDTH]
    v = xs[:, 2 * RW_WIDTH:3 * RW_WIDTH]
    o1 = 3 * RW_WIDTH
    xw = xs[:, o1:o1 + LORA_W]
    xa = xs[:, o1 + LORA_W:o1 + 2 * LORA_W]
    xg = xs[:, o1 + 2 * LORA_W:o1 + 2 * LORA_W + GATE_W]
    bd = bd_ref[...]
    dot_hi = functools.partial(jnp.dot, precision=HIGHEST, preferred_element_type=F32)
    w = -_softplus(-(w0_ref[...] + dot_hi(jnp.tanh(xw), w2_ref[...]))) - 0.5
    logw = -jnp.exp(w)
    a = _sigmoid(a0_ref[...] + dot_hi(xa, a2_ref[...]))
    g = dot_hi(_sigmoid(xg), g2_ref[...])
    kk = k * kk_ref[...]
    kk = kk * lax.rsqrt(jnp.maximum(_seg_sum(kk * kk, bd), 1e-12))
    k = k * (1.0 + (a - 1.0) * ka_ref[...])
    b = kk * a
    k_in, v_in = k, v
    if n_steps < tc:
        live = (lax.broadcasted_iota(jnp.int32, (tc, 1), 0) < n_steps).astype(F32)
        logw, kk, b, k_in, v_in = logw * live, kk * live, b * live, k * live, v * live

    ti = lax.broadcasted_iota(jnp.int32, (tc, tc), 0)
    si = lax.broadcasted_iota(jnp.int32, (tc, tc), 1)
    incl = si <= ti
    strict = si < ti
    eye = (si == ti).astype(F32)
    cl = dot_hi(incl.astype(F32), logw)
    e_pos = jnp.exp(cl)
    e_neg = jnp.exp(-cl)
    at = -(kk * jnp.exp(cl - logw))
    bt = b * e_neg
    kt = k_in * e_neg
    rt = r * e_pos
    w_end = e_pos[tc - 1:tc, :]
    bh = bt * w_end
    kh = kt * w_end
    levels = []
    half = 1
    while half < tc:
        same = (ti >> half.bit_length()) == (si >> half.bit_length())
        levels.append(jnp.where(same, jnp.where((ti & half) != 0, jnp.where((si & half) == 0, 1.0, 0.0), 0.0), 0.0))
        half *= 2
    nt = (((1,), (1,)), ((), ()))
    nn = (((1,), (0,)), ((), ()))
    tn = (((0,), (0,)), ((), ()))
    t2 = lax.broadcasted_iota(jnp.int32, (tc, 2 * tc), 0)
    s2 = lax.broadcasted_iota(jnp.int32, (tc, 2 * tc), 1)
    right_strict = jnp.where(s2 >= tc, jnp.where(s2 - tc < t2, 1.0, 0.0), 0.0)
    both_incl = jnp.where((s2 & (tc - 1)) <= t2, 1.0, 0.0)
    heads = range(N_HEADS)
    sls = [slice(h * HEAD_DIM, (h + 1) * HEAD_DIM) for h in heads]
    s0 = [st_ref[h] for h in heads]
    x2 = [_split(jnp.concatenate([at[:, sl], rt[:, sl]], axis=0)) for sl in sls]
    p = [_dot3(x2[h], _split(jnp.concatenate([bt[:, sls[h]], kt[:, sls[h]]], axis=0)), nt)
         for h in heads]
    n = [jnp.where(strict, p[h][0:tc, 0:tc], 0.0) for h in heads]
    t_inv = [eye + n[h] * levels[0] for h in heads]
    for lv in levels[1:]:
        ts = [_split(t) for t in t_inv]
        x = [_split(_dot3(_split(n[h] * lv), ts[h], nn)) for h in heads]
        t_inv = [t_inv[h] + _dot3(ts[h], x[h], nn) for h in heads]
    ps = [_dot3(x2[h], _split(s0[h]), nt) for h in heads]
    vh = [v_in[:, sl] for sl in sls]
    rhs = [ps[h][0:tc] + _dot3(_split(p[h][0:tc] * right_strict), _split(jnp.concatenate([vh[h], vh[h]], axis=0)), nn)
           for h in heads]
    u = [_dot3(_split(t_inv[h]), _split(rhs[h]), nn) for h in heads]
    uv = [_split(jnp.concatenate([u[h], vh[h]], axis=0)) for h in heads]
    for h in heads:
        y_s[:, sls[h]] = ps[h][tc:2 * tc] + _dot3(_split(p[h][tc:2 * tc] * both_incl), uv[h], nn)
        st_ref[h] = (s0[h] * w_end[:, sls[h]]
                     + _dot3(uv[h], _split(jnp.concatenate([bh[:, sls[h]], kh[:, sls[h]]], axis=0)), tn))

    y = y_s[...]
    mean = _seg_sum(y, bd) * (1.0 / HEAD_DIM)
    yc = y - mean
    var = _seg_sum(yc * yc, bd) * (1.0 / HEAD_DIM)
    y = yc * lax.rsqrt(var + GN_EPS) * gnw_ref[...] + gnb_ref[...]
    y = y + _seg_sum(r * k * rk_ref[...], bd) * v
    o_ref[0] = y * g

    @pl.when(c == pl.num_programs(1) - 1)
    def _():
        st_out_ref[0] = st_ref[...]


def _rwkv(p_rw, shift0, s0, lw, bd, tc, n_steps):
    b, t, _ = p_rw.shape
    nc = t // tc
    assert n_steps == tc or nc == 1
    vec = lambda a: a.reshape(1, -1).astype(F32)
    params = [vec(lw['mu']), vec(lw['w0']), lw['w2'], vec(lw['a0']), lw['a2'], lw['g2'], vec(lw['k_k']),
              vec(lw['k_a']), vec(lw['r_k']), vec(lw['gn_w']), vec(lw['gn_b']), bd]
    full = lambda a: pl.BlockSpec(a.shape, lambda i, c: (0,) * a.ndim)
    return pl.pallas_call(
        functools.partial(_rwkv_kernel, n_steps=n_steps),
        grid=(b, nc),
        in_specs=[pl.BlockSpec((1, tc, SHIFT_WIDTH), lambda i, c: (i, c, 0)),
                  pl.BlockSpec((1, 1, SHIFT_WIDTH), lambda i, c: (i, 0, 0)),
                  pl.BlockSpec((1, N_HEADS, HEAD_DIM, HEAD_DIM), lambda i, c: (i, 0, 0, 0))]
                 + [full(a) for a in params],
        out_specs=[pl.BlockSpec((1, tc, RW_WIDTH), lambda i, c: (i, c, 0)),
                   pl.BlockSpec((1, N_HEADS, HEAD_DIM, HEAD_DIM), lambda i, c: (i, 0, 0, 0))],
        out_shape=[jax.ShapeDtypeStruct((b, t, RW_WIDTH), F32),
                   jax.ShapeDtypeStruct((b, N_HEADS, HEAD_DIM, HEAD_DIM), F32)],
        scratch_shapes=[pltpu.VMEM((N_HEADS, HEAD_DIM, HEAD_DIM), F32),
                        pltpu.VMEM((1, SHIFT_WIDTH), F32), pltpu.VMEM((tc, RW_WIDTH), F32)],
        compiler_params=_cparams(("parallel", "arbitrary")),
        name="rwkv7",
    )(p_rw, shift0, s0, *params)


def _sb_prompt_kernel(q_ref, k_ref, v_ref, b_ref, one_ref, tri_ref, o_ref, acc_ref, *, tq, tk, pad, group):
    i = pl.program_id(1)
    q = q_ref[...]
    lane = lax.broadcasted_iota(jnp.int32, (1, 2 * HEAD_DIM), 1)
    halves = (lane < HEAD_DIM, lane >= HEAD_DIM)
    qh = [jnp.where(halves[h], q, b_ref[h]) for h in range(2)]
    kone = [one_ref[h] for h in range(2)]
    tri = tri_ref[...]
    causal = (lax.broadcasted_iota(jnp.int32, (tq, tk), 1) < lax.broadcasted_iota(jnp.int32, (tq, tk), 0))
    nt = (((1,), (1,)), ((), ()))
    acc_ref[...] = jnp.zeros_like(acc_ref)

    def tiles(js, carry, diagonal):
        ch = [(t, h) for t in range(len(js)) for h in range(2)]
        offs = [pl.multiple_of(j * tk, tk) for j in js]
        ks = [k_ref[pl.ds(o, tk), :] for o in offs]
        vs = [v_ref[pl.ds(o, tk), :] for o in offs]
        z = [lax.dot_general(qh[h], jnp.where(halves[h], ks[t], kone[h]), nt, preferred_element_type=F32)
             for t, h in ch]
        sp = [_softplus2(x) for x in z]
        spm = [jnp.where(causal, x, 0.0) for x in sp] if diagonal else sp
        rows = [jnp.sum(x, axis=1, keepdims=True) for x in spm]
        cs = []
        cur = list(carry)
        for t, h in ch:
            cs.append(cur[h])
            cur[h] = cur[h] + rows[2 * t + h]
        after = [jnp.dot(spm[c].astype(BF16), tri, preferred_element_type=F32) + cs[c] for c in range(len(ch))]
        a = [jnp.exp2((z[c] - sp[c]) - after[c]) for c in range(len(ch))]
        if diagonal:
            a = [jnp.where(causal, x, 0.0) for x in a]
        out = None
        for c, (t, h) in enumerate(ch):
            vh = jnp.where(halves[h], vs[t], jnp.zeros_like(vs[t]))
            d = jnp.dot(a[c].astype(BF16), vh, preferred_element_type=F32)
            out = d if out is None else out + d
        acc_ref[...] += out
        return tuple(cur)

    zero = jnp.zeros((tq, 1), F32)
    carry = tiles([i], (zero, zero), True)
    n_rest = jnp.maximum(i - pad // tk, 0)
    n_group = lax.div(n_rest, group)
    carry = lax.fori_loop(0, n_group,
                          lambda m, c: tiles([i - 1 - group * m - t for t in range(group)], c, False), carry)
    j_next = i - 1 - group * n_group
    lax.fori_loop(0, n_rest - group * n_group, lambda m, c: tiles([j_next - m], c, False), carry)
    o_ref[...] = acc_ref[...]


def _sb_prompt(qb, kb, vb, bias, tq, tk, pad, group):
    t = qb.shape[0]
    assert tq == tk, "the causal mask is applied on the diagonal tile only"
    n_pairs = N_HEADS // 2
    tri = (jnp.arange(tk)[:, None] > jnp.arange(tk)[None, :]).astype(BF16)
    lane = jnp.arange(2 * HEAD_DIM)
    spare = (((jnp.arange(N_HEADS) % 2 == 0) * HEAD_DIM)[:, None])
    b_hi, b_lo = _split(bias.astype(F32) * LOG2E)
    bias_l = (jnp.where(lane[None, :] == spare, b_hi[:, None], 0)
              + jnp.where(lane[None, :] == spare + 1, b_lo[:, None], 0)).astype(BF16).reshape(N_HEADS, 1, 2 * HEAD_DIM)
    ones_l = ((lane[None, :] == spare[:2]) | (lane[None, :] == spare[:2] + 1)).astype(BF16).reshape(2, 1, 2 * HEAD_DIM)
    return pl.pallas_call(
        functools.partial(_sb_prompt_kernel, tq=tq, tk=tk, pad=pad, group=group),
        grid=(n_pairs, t // tq),
        in_specs=[pl.BlockSpec((tq, 2 * HEAD_DIM), lambda p, i: (i, p)),
                  pl.BlockSpec((t, 2 * HEAD_DIM), lambda p, i: (0, p)),
                  pl.BlockSpec((t, 2 * HEAD_DIM), lambda p, i: (0, p)),
                  pl.BlockSpec((2, 1, 2 * HEAD_DIM), lambda p, i: (p, 0, 0)),
                  pl.BlockSpec((2, 1, 2 * HEAD_DIM), lambda p, i: (0, 0, 0)),
                  pl.BlockSpec((tk, tk), lambda p, i: (0, 0))],
        out_specs=pl.BlockSpec((tq, 2 * HEAD_DIM), lambda p, i: (i, p)),
        out_shape=jax.ShapeDtypeStruct((t, D_MODEL), F32),
        scratch_shapes=[pltpu.VMEM((tq, 2 * HEAD_DIM), F32)],
        compiler_params=_cparams(("parallel", "parallel")),
        name="sb_prompt",
    )(qb, kb, vb, bias_l, ones_l, tri)


def _sb_sample_kernel(pt_ref, q_ref, kn_ref, vn_ref, *rest, n_slots, s_new):
    del pt_ref
    kt_refs, vt_refs = rest[0:n_slots], rest[n_slots:2 * n_slots]
    b_ref, tri_ref, dmask_ref, o_ref, acc_ref, carry_ref = rest[2 * n_slots:]
    step = pl.program_id(1)
    q = q_ref[0]
    bias = b_ref[...]
    tri = tri_ref[...]
    nt = (((1,), (1,)), ((), ()))

    def weights(z, mask):
        sp = [_softplus(x) for x in z]
        spm = sp if mask is None else [jnp.where(mask, x, 0.0) for x in sp]
        carry = carry_ref[...]
        a = []
        for t in range(len(z)):
            after = jnp.dot(spm[t], tri, precision=HIGHEST, preferred_element_type=F32) + carry
            x = jnp.exp((z[t] - sp[t]) - after)
            a.append((x if mask is None else jnp.where(mask, x, 0.0)).astype(BF16))
            carry = carry + jnp.sum(spm[t], axis=1, keepdims=True)
        carry_ref[...] = carry
        return a

    @pl.when(step == 0)
    def _():
        acc_ref[...] = jnp.zeros_like(acc_ref)
        carry_ref[...] = jnp.zeros_like(carry_ref)
        z = lax.dot_general(q, kn_ref[0].astype(BF16), nt, preferred_element_type=F32) + bias
        kidx = lax.broadcasted_iota(jnp.int32, z.shape, 1)
        qidx = lax.broadcasted_iota(jnp.int32, z.shape, 0) // N_HEADS
        a, = weights([z], kidx < qidx)
        acc_ref[...] += jnp.dot(a, vn_ref[0].astype(BF16), preferred_element_type=F32)

    z = [jnp.dot(q, kt_refs[t][0].astype(BF16), preferred_element_type=F32) + bias for t in range(n_slots)]
    a = weights(z, None)
    out = None
    for t in range(n_slots):
        d = lax.dot_general(a[t], vt_refs[t][0].astype(BF16), nt, preferred_element_type=F32)
        out = d if out is None else out + d
    acc_ref[...] += out

    @pl.when(step == pl.num_programs(1) - 1)
    def _():
        sel = acc_ref[...] * dmask_ref[...]
        for qi in range(s_new):
            o_ref[0, qi:qi + 1, :] = jnp.sum(sel[qi * N_HEADS:(qi + 1) * N_HEADS, :], axis=0, keepdims=True)


def _sb_sample(qn, kn, v, cache_k, cache_v, page_table, bias):
    db, s_new, _ = qn.shape
    n_phys, psize = cache_k.shape[0], cache_k.shape[1]
    n_pages = page_table.shape[1]
    n_slots = SAMPLE_PAGES_PER_STEP
    assert n_pages % n_slots == 0
    nrow = s_new * N_HEADS
    kt = cache_k.transpose(0, 2, 3, 1).reshape(n_phys, D_MODEL, psize)
    vt = cache_v.transpose(0, 2, 3, 1).reshape(n_phys, D_MODEL, psize)
    head_of_lane = jnp.arange(D_MODEL) // HEAD_DIM
    row_head = jnp.arange(nrow) % N_HEADS
    dmask = (row_head[:, None] == head_of_lane[None, :]).astype(F32)
    qbd = (jnp.repeat(qn, N_HEADS, axis=1) * dmask[None]).astype(BF16)
    padn = lambda a: jnp.pad(a, ((0, 0), (0, psize - s_new), (0, 0)))
    bias_r = bias.astype(F32)[row_head].reshape(nrow, 1)
    tri = (jnp.arange(psize)[:, None] > jnp.arange(psize)[None, :]).astype(F32)
    const = lambda a: pl.BlockSpec(a.shape, lambda b, j, pt: (0, 0))
    page = lambda s: pl.BlockSpec((1, D_MODEL, psize),
                                  lambda b, j, pt: (pt[b, n_pages - 1 - (j * n_slots + s)], 0, 0))
    grid_spec = pltpu.PrefetchScalarGridSpec(
        num_scalar_prefetch=1,
        grid=(db, n_pages // n_slots),
        in_specs=[pl.BlockSpec((1, nrow, D_MODEL), lambda b, j, pt: (b, 0, 0)),
                  pl.BlockSpec((1, psize, D_MODEL), lambda b, j, pt: (b, 0, 0)),
                  pl.BlockSpec((1, psize, D_MODEL), lambda b, j, pt: (b, 0, 0))]
                 + [page(s) for s in range(n_slots)] * 2
                 + [const(bias_r), const(tri), const(dmask)],
        out_specs=pl.BlockSpec((1, s_new, D_MODEL), lambda b, j, pt: (b, 0, 0)),
        scratch_shapes=[pltpu.VMEM((nrow, D_MODEL), F32), pltpu.VMEM((nrow, 1), F32)],
    )
    return pl.pallas_call(
        functools.partial(_sb_sample_kernel, n_slots=n_slots, s_new=s_new),
        grid_spec=grid_spec,
        out_shape=jax.ShapeDtypeStruct((db, s_new, D_MODEL), F32),
        compiler_params=_cparams(("parallel", "arbitrary")),
        name="sb_sample",
    )(page_table, qbd, padn(kn), padn(v), *([kt] * n_slots), *([vt] * n_slots), bias_r, tri, dmask)


def _merge_kernel(x_ref, oa_ref, ob_ref, gs_ref, w_ref, o_ref):
    gs = gs_ref[...]
    merged = gs[:, 0:D_MODEL] * oa_ref[...] + gs[:, D_MODEL:2 * D_MODEL] * ob_ref[...]
    o_ref[...] = x_ref[...] + jnp.dot(merged.astype(BF16), w_ref[...], preferred_element_type=F32)


def _merge(x, oa, ob, gs, w_bf, tm):
    t = x.shape[0]
    row = lambda w: pl.BlockSpec((tm, w), lambda i: (i, 0))
    return pl.pallas_call(
        _merge_kernel,
        grid=(t // tm,),
        in_specs=[row(D_MODEL), row(D_MODEL), row(D_MODEL), row(2 * D_MODEL),
                  pl.BlockSpec((D_MODEL, D_MODEL), lambda i: (0, 0))],
        out_specs=row(D_MODEL),
        out_shape=jax.ShapeDtypeStruct((t, D_MODEL), F32),
        compiler_params=_cparams(("parallel",)),
        name="merge_out",
    )(x, oa, ob, gs, w_bf)


def _top16(s):
    n = s.shape[0]
    row = lax.broadcasted_iota(jnp.int32, s.shape, 0).astype(F32)
    rank = jnp.full(s.shape, float(PEER_TOPK), F32)
    vals = []
    for it in range(PEER_TOPK):
        m = jnp.max(s, axis=0, keepdims=True)
        first = jnp.min(jnp.where(s == m, row, float(n)), axis=0, keepdims=True)
        hit = row == first
        rank = jnp.where(hit, float(it), rank)
        s = jnp.where(hit, -jnp.inf, s)
        vals.append(m)
    return jnp.concatenate(vals, axis=0), rank


def _peer_route_kernel(h_ref, g_ref, wq_ref, sk1_ref, sk2_ref,
                       xn_ref, c1_ref, e1_ref, r2_ref, e2_ref):
    xn = _rms_rows(h_ref[...], g_ref[...])
    xn_ref[...] = xn.astype(BF16)
    q = jnp.dot(xn, wq_ref[...], precision=HIGHEST, preferred_element_type=F32)
    nt = (((1,), (1,)), ((), ()))
    for h in range(PEER_HEADS):
        q1 = q[:, h * 2 * PEER_HALF:h * 2 * PEER_HALF + PEER_HALF]
        q2 = q[:, h * 2 * PEER_HALF + PEER_HALF:(h + 1) * 2 * PEER_HALF]
        s1 = lax.dot_general(sk1_ref[h], q1, nt, precision=HIGHEST, preferred_element_type=F32)
        s2 = lax.dot_general(sk2_ref[h], q2, nt, precision=HIGHEST, preferred_element_type=F32)
        v1, rank1 = _top16(s1)
        v2, rank2 = _top16(s2)
        cand = jnp.concatenate([v1[a:a + 1, :] + v2 for a in range(PEER_TOPK)], axis=0)
        crow = lax.broadcasted_iota(jnp.int32, cand.shape, 0).astype(F32)
        work = cand
        taken = jnp.zeros(cand.shape, F32)
        for _ in range(PEER_TOPK):
            m = jnp.max(work, axis=0, keepdims=True)
            first = jnp.min(jnp.where(work == m, crow, float(PEER_TOPK * PEER_TOPK)), axis=0, keepdims=True)
            hit = crow == first
            taken = jnp.where(hit, 1.0, taken)
            work = jnp.where(hit, -jnp.inf, work)
        top = cand[0:1, :]
        zsum = jnp.sum(taken * jnp.exp(cand - top), axis=0, keepdims=True)
        c1 = jnp.zeros(s1.shape, F32)
        for a in range(PEER_TOPK):
            cnt = jnp.sum(taken[a * PEER_TOPK:(a + 1) * PEER_TOPK, :], axis=0, keepdims=True)
            c1 = jnp.where(rank1 == float(a), cnt, c1)
        c1_ref[h] = c1
        e1_ref[h] = jnp.exp(s1 - v1[0:1, :]) / zsum
        r2_ref[h] = rank2
        e2_ref[h] = jnp.exp(s2 - v2[0:1, :])


def _peer_route(h, g, wq, sk1, sk2, tn, row_off=0):
    t = h.shape[0] - row_off
    blk0 = row_off // tn
    full = lambda a: pl.BlockSpec(a.shape, lambda i: (0,) * a.ndim)
    tab = pl.BlockSpec((PEER_HEADS, PEER_NKEYS, tn), lambda i: (0, 0, i))
    tab_shape = jax.ShapeDtypeStruct((PEER_HEADS, PEER_NKEYS, t), F32)
    return pl.pallas_call(
        _peer_route_kernel,
        grid=(t // tn,),
        in_specs=[pl.BlockSpec((tn, D_MODEL), lambda i: (i + blk0, 0)), full(g), full(wq), full(sk1), full(sk2)],
        out_specs=[pl.BlockSpec((tn, D_MODEL), lambda i: (i, 0)), tab, tab, tab, tab],
        out_shape=[jax.ShapeDtypeStruct((t, D_MODEL), BF16), tab_shape, tab_shape, tab_shape, tab_shape],
        compiler_params=_cparams(("parallel",)),
        name="peer_route",
    )(h, g, wq, sk1, sk2)


def _peer_expert_kernel(h_ref, xn_ref, c1_ref, e1_ref, r2_ref, e2_ref, u_ref, v_ref, o_ref, acc_ref, *, n_i):
    e = pl.program_id(1)

    @pl.when(e == 0)
    def _():
        acc_ref[...] = jnp.zeros_like(acc_ref)

    nt = (((1,), (1,)), ((), ()))
    tn = (((0,), (0,)), ((), ()))
    heads = range(PEER_HEADS)
    xn = xn_ref[...]
    r2 = [r2_ref[h].astype(BF16) for h in heads]
    e2 = [e2_ref[h].astype(BF16) for h in heads]
    rows_per = n_i // PEER_SLICES * PEER_NKEYS
    slices = [slice(s * rows_per, (s + 1) * rows_per) for s in range(PEER_SLICES)]
    ht = [lax.dot_general(u_ref[sl, :], xn, nt, preferred_element_type=F32) for sl in slices]
    gh = []
    for s in range(PEER_SLICES):
        act = 0.5 * ht[s] * (1.0 + lax.erf(ht[s] * (2.0 ** -0.5)))
        rows = []
        for ii in range(s * (n_i // PEER_SLICES), (s + 1) * (n_i // PEER_SLICES)):
            gate = None
            for h in heads:
                c1 = c1_ref[h, ii:ii + 1, :].astype(BF16)
                e1 = e1_ref[h, ii:ii + 1, :].astype(BF16)
                term = jnp.where(r2[h] < c1, e2[h] * e1, jnp.zeros_like(e2[h]))
                gate = term if gate is None else gate + term
            rows.append(gate)
        gh.append(jnp.concatenate(rows, axis=0) * act.astype(BF16))
    out = None
    for s in range(PEER_SLICES):
        d = lax.dot_general(gh[s], v_ref[slices[s], :], tn, preferred_element_type=F32)
        out = d if out is None else out + d
    acc_ref[...] += out

    @pl.when(e == pl.num_programs(1) - 1)
    def _():
        o_ref[...] = h_ref[...] + acc_ref[...]


def _peer_experts(h, xn, c1, e1, r2, e2, u_bf, v_bf, tn, n_i, row_off=0):
    t = xn.shape[0]
    n_exp = u_bf.shape[0]
    eb = n_i * PEER_NKEYS
    blk0 = row_off // tn
    tab_i = pl.BlockSpec((PEER_HEADS, n_i, tn), lambda i, e: (0, e, i))
    tab_j = pl.BlockSpec((PEER_HEADS, PEER_NKEYS, tn), lambda i, e: (0, 0, i))
    return pl.pallas_call(
        functools.partial(_peer_expert_kernel, n_i=n_i),
        grid=(t // tn, n_exp // eb),
        in_specs=[pl.BlockSpec((tn, D_MODEL), lambda i, e: (i + blk0, 0)),
                  pl.BlockSpec((tn, D_MODEL), lambda i, e: (i, 0)),
                  tab_i, tab_i, tab_j, tab_j,
                  pl.BlockSpec((eb, D_MODEL), lambda i, e: (e, 0)),
                  pl.BlockSpec((eb, D_MODEL), lambda i, e: (e, 0))],
        out_specs=pl.BlockSpec((tn, D_MODEL), lambda i, e: (i, 0)),
        out_shape=jax.ShapeDtypeStruct((t, D_MODEL), F32),
        scratch_shapes=[pltpu.VMEM((tn, D_MODEL), F32)],
        compiler_params=_cparams(("parallel", "arbitrary")),
        name="peer_experts",
    )(h, xn, c1, e1, r2, e2, u_bf, v_bf)


PROMPT_PAD = 512 - N_META
ROW_TILE = 256
SCAN_CHUNK = 64
ATT_TQ = 256
ATT_TK = 256
ATT_GROUP = 4
PEER_TN = 512
PEER_NI = 8
PEER_SLICES = 4
SAMPLE_PAGES_PER_STEP = 4


def kernel(x_prompt, x_sample, cache_k, cache_v, page_table, state_shift, state_wkv, meta_tokens, norm_mix_g, w_in, shift_mu, decay_w0, decay_w2, iclr_a0, iclr_a2, gate_g2, k_k, k_a, r_k, gn_w, gn_b, q_norm_g, k_norm_g, sb_bias, w_out, norm_ffn_g, peer_wq, peer_subkeys1, peer_subkeys2, peer_u, peer_v):
    depth = w_in.shape[0]
    assert depth == 1 and x_prompt.shape[0] == 1
    l = 0
    seq = x_prompt.shape[1]
    db, s_new, _ = x_sample.shape
    bd = _seg_ones()
    lw = {'mu': shift_mu[l], 'w0': decay_w0[l], 'w2': decay_w2[l], 'a0': iclr_a0[l], 'a2': iclr_a2[l],
          'g2': gate_g2[l], 'k_k': k_k[l], 'k_a': k_a[l], 'r_k': r_k[l], 'gn_w': gn_w[l], 'gn_b': gn_b[l]}
    g_mix = norm_mix_g[l].reshape(1, D_MODEL)
    g_ffn = norm_ffn_g[l].reshape(1, D_MODEL)
    w_rw = w_in[l][:, :SHIFT_WIDTH].astype(BF16)
    w_sb = w_in[l][:, SHIFT_WIDTH:].astype(BF16)
    qg = jnp.tile(q_norm_g[l], N_HEADS).reshape(1, D_MODEL)
    kg = jnp.tile(k_norm_g[l], N_HEADS).reshape(1, D_MODEL)
    w_o = w_out[l].astype(BF16)
    u_bf = peer_u[l].astype(BF16)
    v_bf = peer_v[l].astype(BF16)

    pad = PROMPT_PAD
    t_all = pad + N_META + seq
    xp = jnp.concatenate([jnp.zeros((pad, D_MODEL), F32), meta_tokens.astype(F32), x_prompt[0]], axis=0)
    p_rw = _proj_rw(xp, g_mix, w_rw, ROW_TILE)
    qb, kn, kb, vv, vb, gs = _proj_sb(xp, g_mix, w_sb, qg, kg, bd, ROW_TILE, LOG2E * HEAD_DIM ** -0.5)
    o_a, wkv_p = _rwkv(p_rw[None], jnp.zeros((1, 1, SHIFT_WIDTH), F32),
                       jnp.zeros((1, N_HEADS, HEAD_DIM, HEAD_DIM), F32), lw, bd, SCAN_CHUNK, SCAN_CHUNK)
    o_b = _sb_prompt(qb, kb, vb, sb_bias[l], ATT_TQ, ATT_TK, pad, ATT_GROUP)
    h_p = _merge(xp, o_a[0], o_b, gs, w_o, ROW_TILE)
    xn_p, c1, e1, r2, e2 = _peer_route(h_p, g_ffn, peer_wq[l], peer_subkeys1[l], peer_subkeys2[l], ROW_TILE,
                                       row_off=pad + N_META)
    y_p = _peer_experts(h_p, xn_p, c1, e1, r2, e2, u_bf, v_bf, PEER_TN, PEER_NI, row_off=pad + N_META)

    k_prompt = kn[pad:].reshape(1, 1, N_META + seq, N_HEADS, HEAD_DIM)
    v_prompt = vv[pad:].reshape(1, 1, N_META + seq, N_HEADS, HEAD_DIM)
    shift_prompt = p_rw[t_all - 1].reshape(1, 1, SHIFT_WIDTH)

    n_s = db * s_new
    xs = x_sample.reshape(n_s, D_MODEL)
    ps_rw = _proj_rw(xs, g_mix, w_rw, n_s)
    qs, kns, _, vs, _, gss = _proj_sb(xs, g_mix, w_sb, qg, kg, bd, n_s, HEAD_DIM ** -0.5)
    ps_pad = jnp.pad(ps_rw.reshape(db, s_new, SHIFT_WIDTH), ((0, 0), (0, SCAN_CHUNK - s_new), (0, 0)))
    o_as, wkv_s = _rwkv(ps_pad, state_shift[l][:, None, :], state_wkv[l], lw, bd, SCAN_CHUNK, s_new)
    o_bs = _sb_sample(qs.astype(F32).reshape(db, s_new, D_MODEL), kns.reshape(db, s_new, D_MODEL),
                      vs.reshape(db, s_new, D_MODEL), cache_k[l], cache_v[l], page_table, sb_bias[l])
    h_s = _merge(xs, o_as[:, :s_new].reshape(n_s, D_MODEL), o_bs.reshape(n_s, D_MODEL), gss, w_o, n_s)
    xn_s, c1s, e1s, r2s, e2s = _peer_route(h_s, g_ffn, peer_wq[l], peer_subkeys1[l], peer_subkeys2[l], n_s)
    y_s = _peer_experts(h_s, xn_s, c1s, e1s, r2s, e2s, u_bf, v_bf, n_s, PEER_NI)

    return (y_p[None], y_s.reshape(db, s_new, D_MODEL), k_prompt, v_prompt, shift_prompt, wkv_p[None],
            kns.reshape(1, db, s_new, N_HEADS, HEAD_DIM), vs.reshape(1, db, s_new, N_HEADS, HEAD_DIM),
            ps_rw.reshape(db, s_new, SHIFT_WIDTH)[:, s_new - 1][None], wkv_s[None])
```

```python
import functools

import jax
import jax.numpy as jnp
from jax import lax
from jax.experimental import pallas as pl
from jax.experimental.pallas import tpu as pltpu

F32 = jnp.float32
BF16 = jnp.bfloat16
HIGHEST = lax.Precision.HIGHEST

D_MODEL = 1024
HEAD_DIM = 64
N_HEADS = 16
N_META = 16
RW_WIDTH = 1024
LORA_W = 64
GATE_W = 128
SHIFT_WIDTH = 3 * RW_WIDTH + 2 * LORA_W + GATE_W
NORM_EPS = 1e-6
LOG2E = 1.4426950408889634
GN_EPS = 64e-5
PEER_HEADS = 8
PEER_NKEYS = 128
PEER_HALF = 128
PEER_TOPK = 16
SEG_BLOCK = 256
VMEM_LIMIT = 56 * 1024 * 1024


def _cparams(sem):
    return pltpu.CompilerParams(dimension_semantics=sem, vmem_limit_bytes=VMEM_LIMIT)


def _seg_ones():
    r = jnp.arange(SEG_BLOCK) // HEAD_DIM
    return (r[:, None] == r[None, :]).astype(F32)


def _seg_sum(x, bd):
    parts = []
    for c in range(x.shape[1] // SEG_BLOCK):
        parts.append(jnp.dot(x[:, c * SEG_BLOCK:(c + 1) * SEG_BLOCK], bd, precision=HIGHEST,
                             preferred_element_type=F32))
    return jnp.concatenate(parts, axis=1)


def _split(x):
    hi = x.astype(BF16)
    return hi, (x - hi.astype(F32)).astype(BF16)


def _dot3(a, b, dims):
    dg = functools.partial(lax.dot_general, dimension_numbers=dims, preferred_element_type=F32)
    return dg(a[0], b[0]) + dg(a[0], b[1]) + dg(a[1], b[0])


def _rms_rows(x, g):
    return x * lax.rsqrt(jnp.mean(x * x, axis=-1, keepdims=True) + NORM_EPS) * g


def _softplus(x):
    sign_bit = jnp.uint32(0x80000000)
    neg_abs = lax.bitcast_convert_type(lax.bitcast_convert_type(x, jnp.uint32) | sign_bit, F32)
    return jnp.maximum(x, 0.0) + jnp.log(1.0 + jnp.exp(neg_abs))


def _softplus2(x):
    sign_bit = jnp.uint32(0x80000000)
    neg_abs = lax.bitcast_convert_type(lax.bitcast_convert_type(x, jnp.uint32) | sign_bit, F32)
    return jnp.maximum(x, 0.0) + jnp.log(1.0 + jnp.exp2(neg_abs)) * LOG2E


def _sigmoid(x):
    return 1.0 / (1.0 + jnp.exp(-x))


def _proj_rw_kernel(x_ref, g_ref, w_ref, o_ref):
    xn = _rms_rows(x_ref[...], g_ref[...])
    o_ref[...] = jnp.dot(xn.astype(BF16), w_ref[...], preferred_element_type=F32)


def _proj_rw(x, g, w_bf, tm):
    t = x.shape[0]
    n = w_bf.shape[1]
    return pl.pallas_call(
        _proj_rw_kernel,
        grid=(t // tm,),
        in_specs=[pl.BlockSpec((tm, D_MODEL), lambda i: (i, 0)),
                  pl.BlockSpec((1, D_MODEL), lambda i: (0, 0)),
                  pl.BlockSpec((D_MODEL, n), lambda i: (0, 0))],
        out_specs=pl.BlockSpec((tm, n), lambda i: (i, 0)),
        out_shape=jax.ShapeDtypeStruct((t, n), F32),
        compiler_params=_cparams(("parallel",)),
        name="proj_rw",
    )(x, g, w_bf)


def _proj_sb_kernel(x_ref, g_ref, w_ref, qg_ref, kg_ref, bd_ref,
                    q_ref, k_ref, kb_ref, v_ref, vb_ref, gs_ref, *, q_scale):
    xn = _rms_rows(x_ref[...], g_ref[...]).astype(BF16)
    p = jnp.dot(xn, w_ref[...], preferred_element_type=F32)
    bd = bd_ref[...]
    q = p[:, 0:D_MODEL]
    k = p[:, D_MODEL:2 * D_MODEL]
    v = p[:, 2 * D_MODEL:3 * D_MODEL]
    qn = q * lax.rsqrt(_seg_sum(q * q, bd) * (1.0 / HEAD_DIM) + NORM_EPS) * qg_ref[...]
    kn = k * lax.rsqrt(_seg_sum(k * k, bd) * (1.0 / HEAD_DIM) + NORM_EPS) * kg_ref[...]
    q_ref[...] = (qn * q_scale).astype(BF16)
    k_ref[...] = kn
    kb_ref[...] = kn.astype(BF16)
    v_ref[...] = v
    vb_ref[...] = v.astype(BF16)
    gs_ref[...] = _sigmoid(p[:, 3 * D_MODEL:5 * D_MODEL])


def _proj_sb(x, g, w_bf, qg, kg, bd, tm, q_scale):
    t = x.shape[0]
    n = w_bf.shape[1]
    row = lambda w: pl.BlockSpec((tm, w), lambda i: (i, 0))
    full = lambda a: pl.BlockSpec(a.shape, lambda i: (0,) * a.ndim)
    return pl.pallas_call(
        functools.partial(_proj_sb_kernel, q_scale=q_scale),
        grid=(t // tm,),
        in_specs=[row(D_MODEL), full(g), full(w_bf), full(qg), full(kg), full(bd)],
        out_specs=[row(D_MODEL), row(D_MODEL), row(D_MODEL), row(D_MODEL), row(D_MODEL), row(2 * D_MODEL)],
        out_shape=[jax.ShapeDtypeStruct((t, D_MODEL), BF16), jax.ShapeDtypeStruct((t, D_MODEL), F32),
                   jax.ShapeDtypeStruct((t, D_MODEL), BF16), jax.ShapeDtypeStruct((t, D_MODEL), F32),
                   jax.ShapeDtypeStruct((t, D_MODEL), BF16), jax.ShapeDtypeStruct((t, 2 * D_MODEL), F32)],
        compiler_params=_cparams(("parallel",)),
        name="proj_sb",
    )(x, g, w_bf, qg, kg, bd)


def _rwkv_kernel(p_ref, sh0_ref, s0_ref, mu_ref, w0_ref, w2_ref, a0_ref, a2_ref, g2_ref, kk_ref, ka_ref,
                 rk_ref, gnw_ref, gnb_ref, bd_ref,
                 o_ref, st_out_ref,
                 st_ref, last_ref, y_s, *, n_steps):
    c = pl.program_id(1)
    tc = p_ref.shape[1]

    @pl.when(c == 0)
    def _():
        st_ref[...] = s0_ref[0]
        last_ref[...] = sh0_ref[0]

    p = p_ref[0]
    row = lax.broadcasted_iota(jnp.int32, p.shape, 0)
    prev = jnp.where(row == 0, last_ref[...], pltpu.roll(p, 1, axis=0))
    last_ref[...] = p[n_steps - 1:n_steps, :]
    xs = p + mu_ref[...] * (prev - p)
    r = xs[:, 0:RW_WIDTH]
    k = xs[:, RW_WIDTH:2 * RW_WIDTH]
    v = xs[:, 2 * RW_WIDTH:3 * RW_WIDTH]
    o1 = 3 * RW_WIDTH
    xw = xs[:, o1:o1 + LORA_W]
    xa = xs[:, o1 + LORA_W:o1 + 2 * LORA_W]
    xg = xs[:, o1 + 2 * LORA_W:o1 + 2 * LORA_W + GATE_W]
    bd = bd_ref[...]
    dot_hi = functools.partial(jnp.dot, precision=HIGHEST, preferred_element_type=F32)
    w = -_softplus(-(w0_ref[...] + dot_hi(jnp.tanh(xw), w2_ref[...]))) - 0.5
    logw = -jnp.exp(w)
    a = _sigmoid(a0_ref[...] + dot_hi(xa, a2_ref[...]))
    g = dot_hi(_sigmoid(xg), g2_ref[...])
    kk = k * kk_ref[...]
    kk = kk * lax.rsqrt(jnp.maximum(_seg_sum(kk * kk, bd), 1e-12))
    k = k * (1.0 + (a - 1.0) * ka_ref[...])
    b = kk * a
    k_in, v_in = k, v
    if n_steps < tc:
        live = (lax.broadcasted_iota(jnp.int32, (tc, 1), 0) < n_steps).astype(F32)
        logw, kk, b, k_in, v_in = logw * live, kk * live, b * live, k * live, v * live

    ti = lax.broadcasted_iota(jnp.int32, (tc, tc), 0)
    si = lax.broadcasted_iota(jnp.int32, (tc, tc), 1)
    incl = si <= ti
    strict = si < ti
    eye = (si == ti).astype(F32)
    cl = dot_hi(incl.astype(F32), logw)
    e_pos = jnp.exp(cl)
    e_neg = jnp.exp(-cl)
    at = -(kk * jnp.exp(cl - logw))
    bt = b * e_neg
    kt = k_in * e_neg
    rt = r * e_pos
    w_end = e_pos[tc - 1:tc, :]
    bh = bt * w_end
    kh = kt * w_end
    levels = []
    half = 1
    while half < tc:
        same = (ti >> half.bit_length()) == (si >> half.bit_length())
        levels.append(jnp.where(same, jnp.where((ti & half) != 0, jnp.where((si & half) == 0, 1.0, 0.0), 0.0), 0.0))
        half *= 2
    nt = (((1,), (1,)), ((), ()))
    nn = (((1,), (0,)), ((), ()))
    tn = (((0,), (0,)), ((), ()))
    t2 = lax.broadcasted_iota(jnp.int32, (tc, 2 * tc), 0)
    s2 = lax.broadcasted_iota(jnp.int32, (tc, 2 * tc), 1)
    right_strict = jnp.where(s2 >= tc, jnp.where(s2 - tc < t2, 1.0, 0.0), 0.0)
    both_incl = jnp.where((s2 & (tc - 1)) <= t2, 1.0, 0.0)
    heads = range(N_HEADS)
    sls = [slice(h * HEAD_DIM, (h + 1) * HEAD_DIM) for h in heads]
    s0 = [st_ref[h] for h in heads]
    x2 = [_split(jnp.concatenate([at[:, sl], rt[:, sl]], axis=0)) for sl in sls]
    p = [_dot3(x2[h], _split(jnp.concatenate([bt[:, sls[h]], kt[:, sls[h]]], axis=0)), nt)
         for h in heads]
    n = [jnp.where(strict, p[h][0:tc, 0:tc], 0.0) for h in heads]
    t_inv = [eye + n[h] * levels[0] for h in heads]
    for lv in levels[1:]:
        ts = [_split(t) for t in t_inv]
        x = [_split(_dot3(_split(n[h] * lv), ts[h], nn)) for h in heads]
        t_inv = [t_inv[h] + _dot3(ts[h], x[h], nn) for h in heads]
    ps = [_dot3(x2[h], _split(s0[h]), nt) for h in heads]
    vh = [v_in[:, sl] for sl in sls]
    rhs = [ps[h][0:tc] + _dot3(_split(p[h][0:tc] * right_strict), _split(jnp.concatenate([vh[h], vh[h]], axis=0)), nn)
           for h in heads]
    u = [_dot3(_split(t_inv[h]), _split(rhs[h]), nn) for h in heads]
    uv = [_split(jnp.concatenate([u[h], vh[h]], axis=0)) for h in heads]
    for h in heads:
        y_s[:, sls[h]] = ps[h][tc:2 * tc] + _dot3(_split(p[h][tc:2 * tc] * both_incl), uv[h], nn)
        st_ref[h] = (s0[h] * w_end[:, sls[h]]
                     + _dot3(uv[h], _split(jnp.concatenate([bh[:, sls[h]], kh[:, sls[h]]], axis=0)), tn))

    y = y_s[...]
    mean = _seg_sum(y, bd) * (1.0 / HEAD_DIM)
    yc = y - mean
    var = _seg_sum(yc * yc, bd) * (1.0 / HEAD_DIM)
    y = yc * lax.rsqrt(var + GN_EPS) * gnw_ref[...] + gnb_ref[...]
    y = y + _seg_sum(r * k * rk_ref[...], bd) * v
    o_ref[0] = y * g

    @pl.when(c == pl.num_programs(1) - 1)
    def _():
        st_out_ref[0] = st_ref[...]


def _rwkv(p_rw, shift0, s0, lw, bd, tc, n_steps):
    b, t, _ = p_rw.shape
    nc = t // tc
    assert n_steps == tc or nc == 1
    vec = lambda a: a.reshape(1, -1).astype(F32)
    params = [vec(lw['mu']), vec(lw['w0']), lw['w2'], vec(lw['a0']), lw['a2'], lw['g2'], vec(lw['k_k']),
              vec(lw['k_a']), vec(lw['r_k']), vec(lw['gn_w']), vec(lw['gn_b']), bd]
    full = lambda a: pl.BlockSpec(a.shape, lambda i, c: (0,) * a.ndim)
    return pl.pallas_call(
        functools.partial(_rwkv_kernel, n_steps=n_steps),
        grid=(b, nc),
        in_specs=[pl.BlockSpec((1, tc, SHIFT_WIDTH), lambda i, c: (i, c, 0)),
                  pl.BlockSpec((1, 1, SHIFT_WIDTH), lambda i, c: (i, 0, 0)),
                  pl.BlockSpec((1, N_HEADS, HEAD_DIM, HEAD_DIM), lambda i, c: (i, 0, 0, 0))]
                 + [full(a) for a in params],
        out_specs=[pl.BlockSpec((1, tc, RW_WIDTH), lambda i, c: (i, c, 0)),
                   pl.BlockSpec((1, N_HEADS, HEAD_DIM, HEAD_DIM), lambda i, c: (i, 0, 0, 0))],
        out_shape=[jax.ShapeDtypeStruct((b, t, RW_WIDTH), F32),
                   jax.ShapeDtypeStruct((b, N_HEADS, HEAD_DIM, HEAD_DIM), F32)],
        scratch_shapes=[pltpu.VMEM((N_HEADS, HEAD_DIM, HEAD_DIM), F32),
                        pltpu.VMEM((1, SHIFT_WIDTH), F32), pltpu.VMEM((tc, RW_WIDTH), F32)],
        compiler_params=_cparams(("parallel", "arbitrary")),
        name="rwkv7",
    )(p_rw, shift0, s0, *params)


def _sb_prompt_kernel(q_ref, k_ref, v_ref, b_ref, one_ref, tri_ref, o_ref, acc_ref, *, tq, tk, pad, group):
    i = pl.program_id(1)
    q = q_ref[...]
    lane = lax.broadcasted_iota(jnp.int32, (1, 2 * HEAD_DIM), 1)
    halves = (lane < HEAD_DIM, lane >= HEAD_DIM)
    qh = [jnp.where(halves[h], q, b_ref[h]) for h in range(2)]
    kone = [one_ref[h] for h in range(2)]
    tri = tri_ref[...]
    causal = (lax.broadcasted_iota(jnp.int32, (tq, tk), 1) < lax.broadcasted_iota(jnp.int32, (tq, tk), 0))
    nt = (((1,), (1,)), ((), ()))
    acc_ref[...] = jnp.zeros_like(acc_ref)

    def tiles(js, carry, diagonal):
        ch = [(t, h) for t in range(len(js)) for h in range(2)]
        offs = [pl.multiple_of(j * tk, tk) for j in js]
        ks = [k_ref[pl.ds(o, tk), :] for o in offs]
        vs = [v_ref[pl.ds(o, tk), :] for o in offs]
        z = [lax.dot_general(qh[h], jnp.where(halves[h], ks[t], kone[h]), nt, preferred_element_type=F32)
             for t, h in ch]
        sp = [_softplus2(x) for x in z]
        spm = [jnp.where(causal, x, 0.0) for x in sp] if diagonal else sp
        rows = [jnp.sum(x, axis=1, keepdims=True) for x in spm]
        cs = []
        cur = list(carry)
        for t, h in ch:
            cs.append(cur[h])
            cur[h] = cur[h] + rows[2 * t + h]
        after = [jnp.dot(spm[c].astype(BF16), tri, preferred_element_type=F32) + cs[c] for c in range(len(ch))]
        a = [jnp.exp2((z[c] - sp[c]) - after[c]) for c in range(len(ch))]
        if diagonal:
            a = [jnp.where(causal, x, 0.0) for x in a]
        out = None
        for c, (t, h) in enumerate(ch):
            vh = jnp.where(halves[h], vs[t], jnp.zeros_like(vs[t]))
            d = jnp.dot(a[c].astype(BF16), vh, preferred_element_type=F32)
            out = d if out is None else out + d
        acc_ref[...] += out
        return tuple(cur)

    zero = jnp.zeros((tq, 1), F32)
    carry = tiles([i], (zero, zero), True)
    n_rest = jnp.maximum(i - pad // tk, 0)
    j_next = i - 1
    for g in sorted({group, 2, 1}, reverse=True):
        n_g = lax.div(n_rest, g)
        carry = lax.fori_loop(
            0, n_g, lambda m, c, g=g, j0=j_next: tiles([j0 - g * m - t for t in range(g)], c, False), carry)
        j_next = j_next - g * n_g
        n_rest = n_rest - g * n_g
    o_ref[...] = acc_ref[...]


def _sb_prompt(qb, kb, vb, bias, tq, tk, pad, group):
    t = qb.shape[0]
    assert tq == tk, "the causal mask is applied on the diagonal tile only"
    n_pairs = N_HEADS // 2
    tri = (jnp.arange(tk)[:, None] > jnp.arange(tk)[None, :]).astype(BF16)
    lane = jnp.arange(2 * HEAD_DIM)
    spare = (((jnp.arange(N_HEADS) % 2 == 0) * HEAD_DIM)[:, None])
    b_hi, b_lo = _split(bias.astype(F32) * LOG2E)
    bias_l = (jnp.where(lane[None, :] == spare, b_hi[:, None], 0)
              + jnp.where(lane[None, :] == spare + 1, b_lo[:, None], 0)).astype(BF16).reshape(N_HEADS, 1, 2 * HEAD_DIM)
    ones_l = ((lane[None, :] == spare[:2]) | (lane[None, :] == spare[:2] + 1)).astype(BF16).reshape(2, 1, 2 * HEAD_DIM)
    return pl.pallas_call(
        functools.partial(_sb_prompt_kernel, tq=tq, tk=tk, pad=pad, group=group),
        grid=(n_pairs, t // tq),
        in_specs=[pl.BlockSpec((tq, 2 * HEAD_DIM), lambda p, i: (i, p)),
                  pl.BlockSpec((t, 2 * HEAD_DIM), lambda p, i: (0, p)),
                  pl.BlockSpec((t, 2 * HEAD_DIM), lambda p, i: (0, p)),
                  pl.BlockSpec((2, 1, 2 * HEAD_DIM), lambda p, i: (p, 0, 0)),
                  pl.BlockSpec((2, 1, 2 * HEAD_DIM), lambda p, i: (0, 0, 0)),
                  pl.BlockSpec((tk, tk), lambda p, i: (0, 0))],
        out_specs=pl.BlockSpec((tq, 2 * HEAD_DIM), lambda p, i: (i, p)),
        out_shape=jax.ShapeDtypeStruct((t, D_MODEL), F32),
        scratch_shapes=[pltpu.VMEM((tq, 2 * HEAD_DIM), F32)],
        compiler_params=_cparams(("parallel", "parallel")),
        name="sb_prompt",
    )(qb, kb, vb, bias_l, ones_l, tri)


def _sb_sample_kernel(pt_ref, q_ref, kn_ref, vn_ref, *rest, n_slots, s_new):
    del pt_ref
    kt_refs, vt_refs = rest[0:n_slots], rest[n_slots:2 * n_slots]
    b_ref, tri_ref, dmask_ref, o_ref, acc_ref, carry_ref = rest[2 * n_slots:]
    step = pl.program_id(1)
    q = q_ref[0]
    bias = b_ref[...]
    tri = tri_ref[...]
    nt = (((1,), (1,)), ((), ()))

    def weights(z, mask):
        sp = [_softplus(x) for x in z]
        spm = sp if mask is None else [jnp.where(mask, x, 0.0) for x in sp]
        carry = carry_ref[...]
        a = []
        for t in range(len(z)):
            after = jnp.dot(spm[t], tri, precision=HIGHEST, preferred_element_type=F32) + carry
            x = jnp.exp((z[t] - sp[t]) - after)
            a.append((x if mask is None else jnp.where(mask, x, 0.0)).astype(BF16))
            carry = carry + jnp.sum(spm[t], axis=1, keepdims=True)
        carry_ref[...] = carry
        return a

    @pl.when(step == 0)
    def _():
        acc_ref[...] = jnp.zeros_like(acc_ref)
        carry_ref[...] = jnp.zeros_like(carry_ref)
        z = lax.dot_general(q, kn_ref[0].astype(BF16), nt, preferred_element_type=F32) + bias
        kidx = lax.broadcasted_iota(jnp.int32, z.shape, 1)
        qidx = lax.broadcasted_iota(jnp.int32, z.shape, 0) // N_HEADS
        a, = weights([z], kidx < qidx)
        acc_ref[...] += jnp.dot(a, vn_ref[0].astype(BF16), preferred_element_type=F32)

    z = [jnp.dot(q, kt_refs[t][0].astype(BF16), preferred_element_type=F32) + bias for t in range(n_slots)]
    a = weights(z, None)
    out = None
    for t in range(n_slots):
        d = lax.dot_general(a[t], vt_refs[t][0].astype(BF16), nt, preferred_element_type=F32)
        out = d if out is None else out + d
    acc_ref[...] += out

    @pl.when(step == pl.num_programs(1) - 1)
    def _():
        sel = acc_ref[...] * dmask_ref[...]
        for qi in range(s_new):
            o_ref[0, qi:qi + 1, :] = jnp.sum(sel[qi * N_HEADS:(qi + 1) * N_HEADS, :], axis=0, keepdims=True)


def _sb_sample(qn, kn, v, cache_k, cache_v, page_table, bias):
    db, s_new, _ = qn.shape
    n_phys, psize = cache_k.shape[0], cache_k.shape[1]
    n_pages = page_table.shape[1]
    n_slots = SAMPLE_PAGES_PER_STEP
    assert n_pages % n_slots == 0
    nrow = s_new * N_HEADS
    kt = cache_k.transpose(0, 2, 3, 1).reshape(n_phys, D_MODEL, psize)
    vt = cache_v.transpose(0, 2, 3, 1).reshape(n_phys, D_MODEL, psize)
    head_of_lane = jnp.arange(D_MODEL) // HEAD_DIM
    row_head = jnp.arange(nrow) % N_HEADS
    dmask = (row_head[:, None] == head_of_lane[None, :]).astype(F32)
    qbd = (jnp.repeat(qn, N_HEADS, axis=1) * dmask[None]).astype(BF16)
    padn = lambda a: jnp.pad(a, ((0, 0), (0, psize - s_new), (0, 0)))
    bias_r = bias.astype(F32)[row_head].reshape(nrow, 1)
    tri = (jnp.arange(psize)[:, None] > jnp.arange(psize)[None, :]).astype(F32)
    const = lambda a: pl.BlockSpec(a.shape, lambda b, j, pt: (0, 0))
    page = lambda s: pl.BlockSpec((1, D_MODEL, psize),
                                  lambda b, j, pt: (pt[b, n_pages - 1 - (j * n_slots + s)], 0, 0))
    grid_spec = pltpu.PrefetchScalarGridSpec(
        num_scalar_prefetch=1,
        grid=(db, n_pages // n_slots),
        in_specs=[pl.BlockSpec((1, nrow, D_MODEL), lambda b, j, pt: (b, 0, 0)),
                  pl.BlockSpec((1, psize, D_MODEL), lambda b, j, pt: (b, 0, 0)),
                  pl.BlockSpec((1, psize, D_MODEL), lambda b, j, pt: (b, 0, 0))]
                 + [page(s) for s in range(n_slots)] * 2
                 + [const(bias_r), const(tri), const(dmask)],
        out_specs=pl.BlockSpec((1, s_new, D_MODEL), lambda b, j, pt: (b, 0, 0)),
        scratch_shapes=[pltpu.VMEM((nrow, D_MODEL), F32), pltpu.VMEM((nrow, 1), F32)],
    )
    return pl.pallas_call(
        functools.partial(_sb_sample_kernel, n_slots=n_slots, s_new=s_new),
        grid_spec=grid_spec,
        out_shape=jax.ShapeDtypeStruct((db, s_new, D_MODEL), F32),
        compiler_params=_cparams(("parallel", "arbitrary")),
        name="sb_sample",
    )(page_table, qbd, padn(kn), padn(v), *([kt] * n_slots), *([vt] * n_slots), bias_r, tri, dmask)


def _merge_kernel(x_ref, oa_ref, ob_ref, gs_ref, w_ref, o_ref):
    gs = gs_ref[...]
    merged = gs[:, 0:D_MODEL] * oa_ref[...] + gs[:, D_MODEL:2 * D_MODEL] * ob_ref[...]
    o_ref[...] = x_ref[...] + jnp.dot(merged.astype(BF16), w_ref[...], preferred_element_type=F32)


def _merge(x, oa, ob, gs, w_bf, tm):
    t = x.shape[0]
    row = lambda w: pl.BlockSpec((tm, w), lambda i: (i, 0))
    return pl.pallas_call(
        _merge_kernel,
        grid=(t // tm,),
        in_specs=[row(D_MODEL), row(D_MODEL), row(D_MODEL), row(2 * D_MODEL),
                  pl.BlockSpec((D_MODEL, D_MODEL), lambda i: (0, 0))],
        out_specs=row(D_MODEL),
        out_shape=jax.ShapeDtypeStruct((t, D_MODEL), F32),
        compiler_params=_cparams(("parallel",)),
        name="merge_out",
    )(x, oa, ob, gs, w_bf)


def _top16(s):
    n = s.shape[0]
    row = lax.broadcasted_iota(jnp.int32, s.shape, 0).astype(F32)
    rank = jnp.full(s.shape, float(PEER_TOPK), F32)
    vals = []
    for it in range(PEER_TOPK):
        m = jnp.max(s, axis=0, keepdims=True)
        first = jnp.min(jnp.where(s == m, row, float(n)), axis=0, keepdims=True)
        hit = row == first
        rank = jnp.where(hit, float(it), rank)
        s = jnp.where(hit, -jnp.inf, s)
        vals.append(m)
    return jnp.concatenate(vals, axis=0), rank


def _peer_route_kernel(h_ref, g_ref, wq_ref, sk1_ref, sk2_ref,
                       xn_ref, c1_ref, e1_ref, r2_ref, e2_ref):
    xn = _rms_rows(h_ref[...], g_ref[...])
    xn_ref[...] = xn.astype(BF16)
    q = jnp.dot(xn, wq_ref[...], precision=HIGHEST, preferred_element_type=F32)
    nt = (((1,), (1,)), ((), ()))
    for h in range(PEER_HEADS):
        q1 = q[:, h * 2 * PEER_HALF:h * 2 * PEER_HALF + PEER_HALF]
        q2 = q[:, h * 2 * PEER_HALF + PEER_HALF:(h + 1) * 2 * PEER_HALF]
        s1 = lax.dot_general(sk1_ref[h], q1, nt, precision=HIGHEST, preferred_element_type=F32)
        s2 = lax.dot_general(sk2_ref[h], q2, nt, precision=HIGHEST, preferred_element_type=F32)
        v1, rank1 = _top16(s1)
        v2, rank2 = _top16(s2)
        cand = jnp.concatenate([v1[a:a + 1, :] + v2 for a in range(PEER_TOPK)], axis=0)
        crow = lax.broadcasted_iota(jnp.int32, cand.shape, 0).astype(F32)
        work = cand
        taken = jnp.zeros(cand.shape, F32)
        for _ in range(PEER_TOPK):
            m = jnp.max(work, axis=0, keepdims=True)
            first = jnp.min(jnp.where(work == m, crow, float(PEER_TOPK * PEER_TOPK)), axis=0, keepdims=True)
            hit = crow == first
            taken = jnp.where(hit, 1.0, taken)
            work = jnp.where(hit, -jnp.inf, work)
        top = cand[0:1, :]
        zsum = jnp.sum(taken * jnp.exp(cand - top), axis=0, keepdims=True)
        c1 = jnp.zeros(s1.shape, F32)
        for a in range(PEER_TOPK):
            cnt = jnp.sum(taken[a * PEER_TOPK:(a + 1) * PEER_TOPK, :], axis=0, keepdims=True)
            c1 = jnp.where(rank1 == float(a), cnt, c1)
        c1_ref[h] = c1
        e1_ref[h] = jnp.exp(s1 - v1[0:1, :]) / zsum
        r2_ref[h] = rank2
        e2_ref[h] = jnp.exp(s2 - v2[0:1, :])


def _peer_route(h, g, wq, sk1, sk2, tn, row_off=0):
    t = h.shape[0] - row_off
    blk0 = row_off // tn
    full = lambda a: pl.BlockSpec(a.shape, lambda i: (0,) * a.ndim)
    tab = pl.BlockSpec((PEER_HEADS, PEER_NKEYS, tn), lambda i: (0, 0, i))
    tab_shape = jax.ShapeDtypeStruct((PEER_HEADS, PEER_NKEYS, t), F32)
    return pl.pallas_call(
        _peer_route_kernel,
        grid=(t // tn,),
        in_specs=[pl.BlockSpec((tn, D_MODEL), lambda i: (i + blk0, 0)), full(g), full(wq), full(sk1), full(sk2)],
        out_specs=[pl.BlockSpec((tn, D_MODEL), lambda i: (i, 0)), tab, tab, tab, tab],
        out_shape=[jax.ShapeDtypeStruct((t, D_MODEL), BF16), tab_shape, tab_shape, tab_shape, tab_shape],
        compiler_params=_cparams(("parallel",)),
        name="peer_route",
    )(h, g, wq, sk1, sk2)


def _peer_expert_kernel(h_ref, xn_ref, c1_ref, e1_ref, r2_ref, e2_ref, u_ref, v_ref, o_ref, acc_ref, *, n_i):
    e = pl.program_id(1)

    @pl.when(e == 0)
    def _():
        acc_ref[...] = jnp.zeros_like(acc_ref)

    nt = (((1,), (1,)), ((), ()))
    tn = (((0,), (0,)), ((), ()))
    heads = range(PEER_HEADS)
    xn = xn_ref[...]
    r2 = [r2_ref[h].astype(BF16) for h in heads]
    e2 = [e2_ref[h].astype(BF16) for h in heads]
    rows_per = n_i // PEER_SLICES * PEER_NKEYS
    slices = [slice(s * rows_per, (s + 1) * rows_per) for s in range(PEER_SLICES)]
    ht = [lax.dot_general(u_ref[sl, :], xn, nt, preferred_element_type=F32) for sl in slices]
    gh = []
    for s in range(PEER_SLICES):
        act = 0.5 * ht[s] * (1.0 + lax.erf(ht[s] * (2.0 ** -0.5)))
        rows = []
        for ii in range(s * (n_i // PEER_SLICES), (s + 1) * (n_i // PEER_SLICES)):
            gate = None
            for h in heads:
                c1 = c1_ref[h, ii:ii + 1, :].astype(BF16)
                e1 = e1_ref[h, ii:ii + 1, :].astype(BF16)
                term = jnp.where(r2[h] < c1, e2[h] * e1, jnp.zeros_like(e2[h]))
                gate = term if gate is None else gate + term
            rows.append(gate)
        gh.append(jnp.concatenate(rows, axis=0) * act.astype(BF16))
    out = None
    for s in range(PEER_SLICES):
        d = lax.dot_general(gh[s], v_ref[slices[s], :], tn, preferred_element_type=F32)
        out = d if out is None else out + d
    acc_ref[...] += out

    @pl.when(e == pl.num_programs(1) - 1)
    def _():
        o_ref[...] = h_ref[...] + acc_ref[...]


def _peer_experts(h, xn, c1, e1, r2, e2, u_bf, v_bf, tn, n_i, row_off=0):
    t = xn.shape[0]
    n_exp = u_bf.shape[0]
    eb = n_i * PEER_NKEYS
    blk0 = row_off // tn
    tab_i = pl.BlockSpec((PEER_HEADS, n_i, tn), lambda i, e: (0, e, i))
    tab_j = pl.BlockSpec((PEER_HEADS, PEER_NKEYS, tn), lambda i, e: (0, 0, i))
    return pl.pallas_call(
        functools.partial(_peer_expert_kernel, n_i=n_i),
        grid=(t // tn, n_exp // eb),
        in_specs=[pl.BlockSpec((tn, D_MODEL), lambda i, e: (i + blk0, 0)),
                  pl.BlockSpec((tn, D_MODEL), lambda i, e: (i, 0)),
                  tab_i, tab_i, tab_j, tab_j,
                  pl.BlockSpec((eb, D_MODEL), lambda i, e: (e, 0)),
                  pl.BlockSpec((eb, D_MODEL), lambda i, e: (e, 0))],
        out_specs=pl.BlockSpec((tn, D_MODEL), lambda i, e: (i, 0)),
        out_shape=jax.ShapeDtypeStruct((t, D_MODEL), F32),
        scratch_shapes=[pltpu.VMEM((tn, D_MODEL), F32)],
        compiler_params=_cparams(("parallel", "arbitrary")),
        name="peer_experts",
    )(h, xn, c1, e1, r2, e2, u_bf, v_bf)


PROMPT_PAD = 512 - N_META
ROW_TILE = 256
SCAN_CHUNK = 64
ATT_TQ = 256
ATT_TK = 256
ATT_GROUP = 8
PEER_TN = 512
PEER_NI = 8
PEER_SLICES = 4
SAMPLE_PAGES_PER_STEP = 8


def kernel(x_prompt, x_sample, cache_k, cache_v, page_table, state_shift, state_wkv, meta_tokens, norm_mix_g, w_in, shift_mu, decay_w0, decay_w2, iclr_a0, iclr_a2, gate_g2, k_k, k_a, r_k, gn_w, gn_b, q_norm_g, k_norm_g, sb_bias, w_out, norm_ffn_g, peer_wq, peer_subkeys1, peer_subkeys2, peer_u, peer_v):
    depth = w_in.shape[0]
    assert depth == 1 and x_prompt.shape[0] == 1
    l = 0
    seq = x_prompt.shape[1]
    db, s_new, _ = x_sample.shape
    bd = _seg_ones()
    lw = {'mu': shift_mu[l], 'w0': decay_w0[l], 'w2': decay_w2[l], 'a0': iclr_a0[l], 'a2': iclr_a2[l],
          'g2': gate_g2[l], 'k_k': k_k[l], 'k_a': k_a[l], 'r_k': r_k[l], 'gn_w': gn_w[l], 'gn_b': gn_b[l]}
    g_mix = norm_mix_g[l].reshape(1, D_MODEL)
    g_ffn = norm_ffn_g[l].reshape(1, D_MODEL)
    w_rw = w_in[l][:, :SHIFT_WIDTH].astype(BF16)
    w_sb = w_in[l][:, SHIFT_WIDTH:].astype(BF16)
    qg = jnp.tile(q_norm_g[l], N_HEADS).reshape(1, D_MODEL)
    kg = jnp.tile(k_norm_g[l], N_HEADS).reshape(1, D_MODEL)
    w_o = w_out[l].astype(BF16)
    u_bf = peer_u[l].astype(BF16)
    v_bf = peer_v[l].astype(BF16)

    pad = PROMPT_PAD
    t_all = pad + N_META + seq
    xp = jnp.concatenate([jnp.zeros((pad, D_MODEL), F32), meta_tokens.astype(F32), x_prompt[0]], axis=0)
    p_rw = _proj_rw(xp, g_mix, w_rw, ROW_TILE)
    qb, kn, kb, vv, vb, gs = _proj_sb(xp, g_mix, w_sb, qg, kg, bd, ROW_TILE, LOG2E * HEAD_DIM ** -0.5)
    o_a, wkv_p = _rwkv(p_rw[None], jnp.zeros((1, 1, SHIFT_WIDTH), F32),
                       jnp.zeros((1, N_HEADS, HEAD_DIM, HEAD_DIM), F32), lw, bd, SCAN_CHUNK, SCAN_CHUNK)
    o_b = _sb_prompt(qb, kb, vb, sb_bias[l], ATT_TQ, ATT_TK, pad, ATT_GROUP)
    h_p = _merge(xp, o_a[0], o_b, gs, w_o, ROW_TILE)
    xn_p, c1, e1, r2, e2 = _peer_route(h_p, g_ffn, peer_wq[l], peer_subkeys1[l], peer_subkeys2[l], ROW_TILE,
                                       row_off=pad + N_META)
    y_p = _peer_experts(h_p, xn_p, c1, e1, r2, e2, u_bf, v_bf, PEER_TN, PEER_NI, row_off=pad + N_META)

    k_prompt = kn[pad:].reshape(1, 1, N_META + seq, N_HEADS, HEAD_DIM)
    v_prompt = vv[pad:].reshape(1, 1, N_META + seq, N_HEADS, HEAD_DIM)
    shift_prompt = p_rw[t_all - 1].reshape(1, 1, SHIFT_WIDTH)

    n_s = db * s_new
    xs = x_sample.reshape(n_s, D_MODEL)
    ps_rw = _proj_rw(xs, g_mix, w_rw, n_s)
    qs, kns, _, vs, _, gss = _proj_sb(xs, g_mix, w_sb, qg, kg, bd, n_s, HEAD_DIM ** -0.5)
    ps_pad = jnp.pad(ps_rw.reshape(db, s_new, SHIFT_WIDTH), ((0, 0), (0, SCAN_CHUNK - s_new), (0, 0)))
    o_as, wkv_s = _rwkv(ps_pad, state_shift[l][:, None, :], state_wkv[l], lw, bd, SCAN_CHUNK, s_new)
    o_bs = _sb_sample(qs.astype(F32).reshape(db, s_new, D_MODEL), kns.reshape(db, s_new, D_MODEL),
                      vs.reshape(db, s_new, D_MODEL), cache_k[l], cache_v[l], page_table, sb_bias[l])
    h_s = _merge(xs, o_as[:, :s_new].reshape(n_s, D_MODEL), o_bs.reshape(n_s, D_MODEL), gss, w_o, n_s)
    xn_s, c1s, e1s, r2s, e2s = _peer_route(h_s, g_ffn, peer_wq[l], peer_subkeys1[l], peer_subkeys2[l], n_s)
    y_s = _peer_experts(h_s, xn_s, c1s, e1s, r2s, e2s, u_bf, v_bf, n_s, PEER_NI)

    return (y_p[None], y_s.reshape(db, s_new, D_MODEL), k_prompt, v_prompt, shift_prompt, wkv_p[None],
            kns.reshape(1, db, s_new, N_HEADS, HEAD_DIM), vs.reshape(1, db, s_new, N_HEADS, HEAD_DIM),
            ps_rw.reshape(db, s_new, SHIFT_WIDTH)[:, s_new - 1][None], wkv_s[None])
```

```python
import functools

import jax
import jax.numpy as jnp
from jax import lax
from jax.experimental import pallas as pl
from jax.experimental.pallas import tpu as pltpu

F32 = jnp.float32
BF16 = jnp.bfloat16
HIGHEST = lax.Precision.HIGHEST

D_MODEL = 1024
HEAD_DIM = 64
N_HEADS = 16
N_META = 16
RW_WIDTH = 1024
LORA_W = 64
GATE_W = 128
SHIFT_WIDTH = 3 * RW_WIDTH + 2 * LORA_W + GATE_W
NORM_EPS = 1e-6
LOG2E = 1.4426950408889634
GN_EPS = 64e-5
PEER_HEADS = 8
PEER_NKEYS = 128
PEER_HALF = 128
PEER_TOPK = 16
SEG_BLOCK = 256
VMEM_LIMIT = 56 * 1024 * 1024


def _cparams(sem):
    return pltpu.CompilerParams(dimension_semantics=sem, vmem_limit_bytes=VMEM_LIMIT)


def _seg_ones():
    r = jnp.arange(SEG_BLOCK) // HEAD_DIM
    return (r[:, None] == r[None, :]).astype(F32)


def _seg_sum(x, bd):
    parts = []
    for c in range(x.shape[1] // SEG_BLOCK):
        parts.append(jnp.dot(x[:, c * SEG_BLOCK:(c + 1) * SEG_BLOCK], bd, precision=HIGHEST,
                             preferred_element_type=F32))
    return jnp.concatenate(parts, axis=1)


def _split(x):
    hi = x.astype(BF16)
    return hi, (x - hi.astype(F32)).astype(BF16)


def _dot3(a, b, dims):
    dg = functools.partial(lax.dot_general, dimension_numbers=dims, preferred_element_type=F32)
    return dg(a[0], b[0]) + dg(a[0], b[1]) + dg(a[1], b[0])


def _rms_rows(x, g):
    return x * lax.rsqrt(jnp.mean(x * x, axis=-1, keepdims=True) + NORM_EPS) * g


def _softplus(x):
    sign_bit = jnp.uint32(0x80000000)
    neg_abs = lax.bitcast_convert_type(lax.bitcast_convert_type(x, jnp.uint32) | sign_bit, F32)
    return jnp.maximum(x, 0.0) + jnp.log(1.0 + jnp.exp(neg_abs))


def _softplus2(x):
    sign_bit = jnp.uint32(0x80000000)
    neg_abs = lax.bitcast_convert_type(lax.bitcast_convert_type(x, jnp.uint32) | sign_bit, F32)
    return jnp.maximum(x, 0.0) + jnp.log(1.0 + jnp.exp2(neg_abs)) * LOG2E


def _sigmoid(x):
    return 1.0 / (1.0 + jnp.exp(-x))


def _proj_rw_kernel(x_ref, g_ref, w_ref, o_ref):
    xn = _rms_rows(x_ref[...], g_ref[...])
    o_ref[...] = jnp.dot(xn.astype(BF16), w_ref[...], preferred_element_type=F32)


def _proj_rw(x, g, w_bf, tm):
    t = x.shape[0]
    n = w_bf.shape[1]
    return pl.pallas_call(
        _proj_rw_kernel,
        grid=(t // tm,),
        in_specs=[pl.BlockSpec((tm, D_MODEL), lambda i: (i, 0)),
                  pl.BlockSpec((1, D_MODEL), lambda i: (0, 0)),
                  pl.BlockSpec((D_MODEL, n), lambda i: (0, 0))],
        out_specs=pl.BlockSpec((tm, n), lambda i: (i, 0)),
        out_shape=jax.ShapeDtypeStruct((t, n), F32),
        compiler_params=_cparams(("parallel",)),
        name="proj_rw",
    )(x, g, w_bf)


def _proj_sb_kernel(x_ref, g_ref, w_ref, qg_ref, kg_ref, bd_ref,
                    q_ref, k_ref, kb_ref, v_ref, vb_ref, gs_ref, *, q_scale):
    xn = _rms_rows(x_ref[...], g_ref[...]).astype(BF16)
    p = jnp.dot(xn, w_ref[...], preferred_element_type=F32)
    bd = bd_ref[...]
    q = p[:, 0:D_MODEL]
    k = p[:, D_MODEL:2 * D_MODEL]
    v = p[:, 2 * D_MODEL:3 * D_MODEL]
    qn = q * lax.rsqrt(_seg_sum(q * q, bd) * (1.0 / HEAD_DIM) + NORM_EPS) * qg_ref[...]
    kn = k * lax.rsqrt(_seg_sum(k * k, bd) * (1.0 / HEAD_DIM) + NORM_EPS) * kg_ref[...]
    q_ref[...] = (qn * q_scale).astype(BF16)
    k_ref[...] = kn
    kb_ref[...] = kn.astype(BF16)
    v_ref[...] = v
    vb_ref[...] = v.astype(BF16)
    gs_ref[...] = _sigmoid(p[:, 3 * D_MODEL:5 * D_MODEL])


def _proj_sb(x, g, w_bf, qg, kg, bd, tm, q_scale):
    t = x.shape[0]
    n = w_bf.shape[1]
    row = lambda w: pl.BlockSpec((tm, w), lambda i: (i, 0))
    full = lambda a: pl.BlockSpec(a.shape, lambda i: (0,) * a.ndim)
    return pl.pallas_call(
        functools.partial(_proj_sb_kernel, q_scale=q_scale),
        grid=(t // tm,),
        in_specs=[row(D_MODEL), full(g), full(w_bf), full(qg), full(kg), full(bd)],
        out_specs=[row(D_MODEL), row(D_MODEL), row(D_MODEL), row(D_MODEL), row(D_MODEL), row(2 * D_MODEL)],
        out_shape=[jax.ShapeDtypeStruct((t, D_MODEL), BF16), jax.ShapeDtypeStruct((t, D_MODEL), F32),
                   jax.ShapeDtypeStruct((t, D_MODEL), BF16), jax.ShapeDtypeStruct((t, D_MODEL), F32),
                   jax.ShapeDtypeStruct((t, D_MODEL), BF16), jax.ShapeDtypeStruct((t, 2 * D_MODEL), F32)],
        compiler_params=_cparams(("parallel",)),
        name="proj_sb",
    )(x, g, w_bf, qg, kg, bd)


def _rwkv_kernel(p_ref, sh0_ref, s0_ref, mu_ref, w0_ref, w2_ref, a0_ref, a2_ref, g2_ref, kk_ref, ka_ref,
                 rk_ref, gnw_ref, gnb_ref, bd_ref,
                 o_ref, st_out_ref,
                 st_ref, last_ref, y_s, *, n_steps):
    c = pl.program_id(1)
    tc = p_ref.shape[1]

    @pl.when(c == 0)
    def _():
        st_ref[...] = s0_ref[0]
        last_ref[...] = sh0_ref[0]

    p = p_ref[0]
    row = lax.broadcasted_iota(jnp.int32, p.shape, 0)
    prev = jnp.where(row == 0, last_ref[...], pltpu.roll(p, 1, axis=0))
    last_ref[...] = p[n_steps - 1:n_steps, :]
    xs = p + mu_ref[...] * (prev - p)
    r = xs[:, 0:RW_WIDTH]
    k = xs[:, RW_WIDTH:2 * RW_WIDTH]
    v = xs[:, 2 * RW_WIDTH:3 * RW_WIDTH]
    o1 = 3 * RW_WIDTH
    xw = xs[:, o1:o1 + LORA_W]
    xa = xs[:, o1 + LORA_W:o1 + 2 * LORA_W]
    xg = xs[:, o1 + 2 * LORA_W:o1 + 2 * LORA_W + GATE_W]
    bd = bd_ref[...]
    dot_hi = functools.partial(jnp.dot, precision=HIGHEST, preferred_element_type=F32)
    w = -_softplus(-(w0_ref[...] + dot_hi(jnp.tanh(xw), w2_ref[...]))) - 0.5
    logw = -jnp.exp(w)
    a = _sigmoid(a0_ref[...] + dot_hi(xa, a2_ref[...]))
    g = dot_hi(_sigmoid(xg), g2_ref[...])
    kk = k * kk_ref[...]
    kk = kk * lax.rsqrt(jnp.maximum(_seg_sum(kk * kk, bd), 1e-12))
    k = k * (1.0 + (a - 1.0) * ka_ref[...])
    b = kk * a
    k_in, v_in = k, v
    if n_steps < tc:
        live = (lax.broadcasted_iota(jnp.int32, (tc, 1), 0) < n_steps).astype(F32)
        logw, kk, b, k_in, v_in = logw * live, kk * live, b * live, k * live, v * live

    ti = lax.broadcasted_iota(jnp.int32, (tc, tc), 0)
    si = lax.broadcasted_iota(jnp.int32, (tc, tc), 1)
    incl = si <= ti
    strict = si < ti
    eye = (si == ti).astype(F32)
    cl = dot_hi(incl.astype(F32), logw)
    e_pos = jnp.exp(cl)
    e_neg = jnp.exp(-cl)
    at = -(kk * jnp.exp(cl - logw))
    bt = b * e_neg
    kt = k_in * e_neg
    rt = r * e_pos
    w_end = e_pos[tc - 1:tc, :]
    bh = bt * w_end
    kh = kt * w_end
    levels = []
    half = 1
    while half < tc:
        same = (ti >> half.bit_length()) == (si >> half.bit_length())
        levels.append(jnp.where(same, jnp.where((ti & half) != 0, jnp.where((si & half) == 0, 1.0, 0.0), 0.0), 0.0))
        half *= 2
    nt = (((1,), (1,)), ((), ()))
    nn = (((1,), (0,)), ((), ()))
    tn = (((0,), (0,)), ((), ()))
    t2 = lax.broadcasted_iota(jnp.int32, (tc, 2 * tc), 0)
    s2 = lax.broadcasted_iota(jnp.int32, (tc, 2 * tc), 1)
    right_strict = jnp.where(s2 >= tc, jnp.where(s2 - tc < t2, 1.0, 0.0), 0.0)
    both_incl = jnp.where((s2 & (tc - 1)) <= t2, 1.0, 0.0)
    heads = range(N_HEADS)
    sls = [slice(h * HEAD_DIM, (h + 1) * HEAD_DIM) for h in heads]
    s0 = [st_ref[h] for h in heads]
    x2 = [_split(jnp.concatenate([at[:, sl], rt[:, sl]], axis=0)) for sl in sls]
    p = [_dot3(x2[h], _split(jnp.concatenate([bt[:, sls[h]], kt[:, sls[h]]], axis=0)), nt)
         for h in heads]
    n = [jnp.where(strict, p[h][0:tc, 0:tc], 0.0) for h in heads]
    t_inv = [eye + n[h] * levels[0] for h in heads]
    for lv in levels[1:]:
        ts = [_split(t) for t in t_inv]
        x = [_split(_dot3(_split(n[h] * lv), ts[h], nn)) for h in heads]
        t_inv = [t_inv[h] + _dot3(ts[h], x[h], nn) for h in heads]
    ps = [_dot3(x2[h], _split(s0[h]), nt) for h in heads]
    vh = [v_in[:, sl] for sl in sls]
    rhs = [ps[h][0:tc] + _dot3(_split(p[h][0:tc] * right_strict), _split(jnp.concatenate([vh[h], vh[h]], axis=0)), nn)
           for h in heads]
    u = [_dot3(_split(t_inv[h]), _split(rhs[h]), nn) for h in heads]
    uv = [_split(jnp.concatenate([u[h], vh[h]], axis=0)) for h in heads]
    for h in heads:
        y_s[:, sls[h]] = ps[h][tc:2 * tc] + _dot3(_split(p[h][tc:2 * tc] * both_incl), uv[h], nn)
        st_ref[h] = (s0[h] * w_end[:, sls[h]]
                     + _dot3(uv[h], _split(jnp.concatenate([bh[:, sls[h]], kh[:, sls[h]]], axis=0)), tn))

    y = y_s[...]
    mean = _seg_sum(y, bd) * (1.0 / HEAD_DIM)
    yc = y - mean
    var = _seg_sum(yc * yc, bd) * (1.0 / HEAD_DIM)
    y = yc * lax.rsqrt(var + GN_EPS) * gnw_ref[...] + gnb_ref[...]
    y = y + _seg_sum(r * k * rk_ref[...], bd) * v
    o_ref[0] = y * g

    @pl.when(c == pl.num_programs(1) - 1)
    def _():
        st_out_ref[0] = st_ref[...]


def _rwkv(p_rw, shift0, s0, lw, bd, tc, n_steps):
    b, t, _ = p_rw.shape
    nc = t // tc
    assert n_steps == tc or nc == 1
    vec = lambda a: a.reshape(1, -1).astype(F32)
    params = [vec(lw['mu']), vec(lw['w0']), lw['w2'], vec(lw['a0']), lw['a2'], lw['g2'], vec(lw['k_k']),
              vec(lw['k_a']), vec(lw['r_k']), vec(lw['gn_w']), vec(lw['gn_b']), bd]
    full = lambda a: pl.BlockSpec(a.shape, lambda i, c: (0,) * a.ndim)
    return pl.pallas_call(
        functools.partial(_rwkv_kernel, n_steps=n_steps),
        grid=(b, nc),
        in_specs=[pl.BlockSpec((1, tc, SHIFT_WIDTH), lambda i, c: (i, c, 0)),
                  pl.BlockSpec((1, 1, SHIFT_WIDTH), lambda i, c: (i, 0, 0)),
                  pl.BlockSpec((1, N_HEADS, HEAD_DIM, HEAD_DIM), lambda i, c: (i, 0, 0, 0))]
                 + [full(a) for a in params],
        out_specs=[pl.BlockSpec((1, tc, RW_WIDTH), lambda i, c: (i, c, 0)),
                   pl.BlockSpec((1, N_HEADS, HEAD_DIM, HEAD_DIM), lambda i, c: (i, 0, 0, 0))],
        out_shape=[jax.ShapeDtypeStruct((b, t, RW_WIDTH), F32),
                   jax.ShapeDtypeStruct((b, N_HEADS, HEAD_DIM, HEAD_DIM), F32)],
        scratch_shapes=[pltpu.VMEM((N_HEADS, HEAD_DIM, HEAD_DIM), F32),
                        pltpu.VMEM((1, SHIFT_WIDTH), F32), pltpu.VMEM((tc, RW_WIDTH), F32)],
        compiler_params=_cparams(("parallel", "arbitrary")),
        name="rwkv7",
    )(p_rw, shift0, s0, *params)


def _sb_prompt_kernel(q_ref, k_ref, v_ref, b_ref, one_ref, tri_ref, o_ref, acc_ref, *, tq, tk, pad, group):
    i = pl.program_id(1)
    q = q_ref[...]
    lane = lax.broadcasted_iota(jnp.int32, (1, 2 * HEAD_DIM), 1)
    halves = (lane < HEAD_DIM, lane >= HEAD_DIM)
    qh = [jnp.where(halves[h], q, b_ref[h]) for h in range(2)]
    kone = [one_ref[h] for h in range(2)]
    tri = tri_ref[...]
    causal = (lax.broadcasted_iota(jnp.int32, (tq, tk), 1) < lax.broadcasted_iota(jnp.int32, (tq, tk), 0))
    nt = (((1,), (1,)), ((), ()))
    acc_ref[...] = jnp.zeros_like(acc_ref)

    def tiles(js, carry, diagonal):
        ch = [(t, h) for t in range(len(js)) for h in range(2)]
        offs = [pl.multiple_of(j * tk, tk) for j in js]
        ks = [k_ref[pl.ds(o, tk), :] for o in offs]
        vs = [v_ref[pl.ds(o, tk), :] for o in offs]
        z = [lax.dot_general(qh[h], jnp.where(halves[h], ks[t], kone[h]), nt, preferred_element_type=F32)
             for t, h in ch]
        sp = [_softplus2(x) for x in z]
        spm = [jnp.where(causal, x, 0.0) for x in sp] if diagonal else sp
        rows = [jnp.sum(x, axis=1, keepdims=True) for x in spm]
        cs = []
        cur = list(carry)
        for t, h in ch:
            cs.append(cur[h])
            cur[h] = cur[h] + rows[2 * t + h]
        after = [jnp.dot(spm[c].astype(BF16), tri, preferred_element_type=F32) + cs[c] for c in range(len(ch))]
        a = [jnp.exp2((z[c] - sp[c]) - after[c]) for c in range(len(ch))]
        if diagonal:
            a = [jnp.where(causal, x, 0.0) for x in a]
        out = None
        for c, (t, h) in enumerate(ch):
            vh = jnp.where(halves[h], vs[t], jnp.zeros_like(vs[t]))
            d = jnp.dot(a[c].astype(BF16), vh, preferred_element_type=F32)
            out = d if out is None else out + d
        acc_ref[...] += out
        return tuple(cur)

    zero = jnp.zeros((tq, 1), F32)
    carry = tiles([i], (zero, zero), True)
    n_rest = jnp.maximum(i - pad // tk, 0)
    j_next = i - 1
    for g in sorted({group, 2, 1}, reverse=True):
        n_g = lax.div(n_rest, g)
        carry = lax.fori_loop(
            0, n_g, lambda m, c, g=g, j0=j_next: tiles([j0 - g * m - t for t in range(g)], c, False), carry)
        j_next = j_next - g * n_g
        n_rest = n_rest - g * n_g
    o_ref[...] = acc_ref[...]


def _sb_prompt(qb, kb, vb, bias, tq, tk, pad, group):
    t = qb.shape[0]
    assert tq == tk, "the causal mask is applied on the diagonal tile only"
    n_pairs = N_HEADS // 2
    tri = (jnp.arange(tk)[:, None] > jnp.arange(tk)[None, :]).astype(BF16)
    lane = jnp.arange(2 * HEAD_DIM)
    spare = (((jnp.arange(N_HEADS) % 2 == 0) * HEAD_DIM)[:, None])
    b_hi, b_lo = _split(bias.astype(F32) * LOG2E)
    bias_l = (jnp.where(lane[None, :] == spare, b_hi[:, None], 0)
              + jnp.where(lane[None, :] == spare + 1, b_lo[:, None], 0)).astype(BF16).reshape(N_HEADS, 1, 2 * HEAD_DIM)
    ones_l = ((lane[None, :] == spare[:2]) | (lane[None, :] == spare[:2] + 1)).astype(BF16).reshape(2, 1, 2 * HEAD_DIM)
    return pl.pallas_call(
        functools.partial(_sb_prompt_kernel, tq=tq, tk=tk, pad=pad, group=group),
        grid=(n_pairs, t // tq),
        in_specs=[pl.BlockSpec((tq, 2 * HEAD_DIM), lambda p, i: (i, p)),
                  pl.BlockSpec((t, 2 * HEAD_DIM), lambda p, i: (0, p)),
                  pl.BlockSpec((t, 2 * HEAD_DIM), lambda p, i: (0, p)),
                  pl.BlockSpec((2, 1, 2 * HEAD_DIM), lambda p, i: (p, 0, 0)),
                  pl.BlockSpec((2, 1, 2 * HEAD_DIM), lambda p, i: (0, 0, 0)),
                  pl.BlockSpec((tk, tk), lambda p, i: (0, 0))],
        out_specs=pl.BlockSpec((tq, 2 * HEAD_DIM), lambda p, i: (i, p)),
        out_shape=jax.ShapeDtypeStruct((t, D_MODEL), F32),
        scratch_shapes=[pltpu.VMEM((tq, 2 * HEAD_DIM), F32)],
        compiler_params=_cparams(("parallel", "parallel")),
        name="sb_prompt",
    )(qb, kb, vb, bias_l, ones_l, tri)


def _sb_sample_kernel(pt_ref, q_ref, kn_ref, vn_ref, *rest, n_slots, s_new):
    del pt_ref
    kt_refs, vt_refs = rest[0:n_slots], rest[n_slots:2 * n_slots]
    b_ref, tri_ref, dmask_ref, o_ref, acc_ref, carry_ref = rest[2 * n_slots:]
    step = pl.program_id(1)
    q = q_ref[0]
    bias = b_ref[...]
    tri = tri_ref[...]
    nt = (((1,), (1,)), ((), ()))

    def weights(z, mask):
        sp = [_softplus(x) for x in z]
        spm = sp if mask is None else [jnp.where(mask, x, 0.0) for x in sp]
        carry = carry_ref[...]
        a = []
        for t in range(len(z)):
            after = jnp.dot(spm[t], tri, precision=HIGHEST, preferred_element_type=F32) + carry
            x = jnp.exp((z[t] - sp[t]) - after)
            a.append((x if mask is None else jnp.where(mask, x, 0.0)).astype(BF16))
            carry = carry + jnp.sum(spm[t], axis=1, keepdims=True)
        carry_ref[...] = carry
        return a

    @pl.when(step == 0)
    def _():
        acc_ref[...] = jnp.zeros_like(acc_ref)
        carry_ref[...] = jnp.zeros_like(carry_ref)
        z = lax.dot_general(q, kn_ref[0].astype(BF16), nt, preferred_element_type=F32) + bias
        kidx = lax.broadcasted_iota(jnp.int32, z.shape, 1)
        qidx = lax.broadcasted_iota(jnp.int32, z.shape, 0) // N_HEADS
        a, = weights([z], kidx < qidx)
        acc_ref[...] += jnp.dot(a, vn_ref[0].astype(BF16), preferred_element_type=F32)

    z = [jnp.dot(q, kt_refs[t][0].astype(BF16), preferred_element_type=F32) + bias for t in range(n_slots)]
    a = weights(z, None)
    out = None
    for t in range(n_slots):
        d = lax.dot_general(a[t], vt_refs[t][0].astype(BF16), nt, preferred_element_type=F32)
        out = d if out is None else out + d
    acc_ref[...] += out

    @pl.when(step == pl.num_programs(1) - 1)
    def _():
        sel = acc_ref[...] * dmask_ref[...]
        for qi in range(s_new):
            o_ref[0, qi:qi + 1, :] = jnp.sum(sel[qi * N_HEADS:(qi + 1) * N_HEADS, :], axis=0, keepdims=True)


def _sb_sample(qn, kn, v, cache_k, cache_v, page_table, bias):
    db, s_new, _ = qn.shape
    n_phys, psize = cache_k.shape[0], cache_k.shape[1]
    n_pages = page_table.shape[1]
    n_slots = SAMPLE_PAGES_PER_STEP
    assert n_pages % n_slots == 0
    nrow = s_new * N_HEADS
    kt = cache_k.transpose(0, 2, 3, 1).reshape(n_phys, D_MODEL, psize)
    vt = cache_v.transpose(0, 2, 3, 1).reshape(n_phys, D_MODEL, psize)
    head_of_lane = jnp.arange(D_MODEL) // HEAD_DIM
    row_head = jnp.arange(nrow) % N_HEADS
    dmask = (row_head[:, None] == head_of_lane[None, :]).astype(F32)
    qbd = (jnp.repeat(qn, N_HEADS, axis=1) * dmask[None]).astype(BF16)
    padn = lambda a: jnp.pad(a, ((0, 0), (0, psize - s_new), (0, 0)))
    bias_r = bias.astype(F32)[row_head].reshape(nrow, 1)
    tri = (jnp.arange(psize)[:, None] > jnp.arange(psize)[None, :]).astype(F32)
    const = lambda a: pl.BlockSpec(a.shape, lambda b, j, pt: (0, 0))
    page = lambda s: pl.BlockSpec((1, D_MODEL, psize),
                                  lambda b, j, pt: (pt[b, n_pages - 1 - (j * n_slots + s)], 0, 0))
    grid_spec = pltpu.PrefetchScalarGridSpec(
        num_scalar_prefetch=1,
        grid=(db, n_pages // n_slots),
        in_specs=[pl.BlockSpec((1, nrow, D_MODEL), lambda b, j, pt: (b, 0, 0)),
                  pl.BlockSpec((1, psize, D_MODEL), lambda b, j, pt: (b, 0, 0)),
                  pl.BlockSpec((1, psize, D_MODEL), lambda b, j, pt: (b, 0, 0))]
                 + [page(s) for s in range(n_slots)] * 2
                 + [const(bias_r), const(tri), const(dmask)],
        out_specs=pl.BlockSpec((1, s_new, D_MODEL), lambda b, j, pt: (b, 0, 0)),
        scratch_shapes=[pltpu.VMEM((nrow, D_MODEL), F32), pltpu.VMEM((nrow, 1), F32)],
    )
    return pl.pallas_call(
        functools.partial(_sb_sample_kernel, n_slots=n_slots, s_new=s_new),
        grid_spec=grid_spec,
        out_shape=jax.ShapeDtypeStruct((db, s_new, D_MODEL), F32),
        compiler_params=_cparams(("parallel", "arbitrary")),
        name="sb_sample",
    )(page_table, qbd, padn(kn), padn(v), *([kt] * n_slots), *([vt] * n_slots), bias_r, tri, dmask)


def _merge_kernel(x_ref, oa_ref, ob_ref, gs_ref, w_ref, o_ref):
    gs = gs_ref[...]
    merged = gs[:, 0:D_MODEL] * oa_ref[...] + gs[:, D_MODEL:2 * D_MODEL] * ob_ref[...]
    o_ref[...] = x_ref[...] + jnp.dot(merged.astype(BF16), w_ref[...], preferred_element_type=F32)


def _merge(x, oa, ob, gs, w_bf, tm):
    t = x.shape[0]
    row = lambda w: pl.BlockSpec((tm, w), lambda i: (i, 0))
    return pl.pallas_call(
        _merge_kernel,
        grid=(t // tm,),
        in_specs=[row(D_MODEL), row(D_MODEL), row(D_MODEL), row(2 * D_MODEL),
                  pl.BlockSpec((D_MODEL, D_MODEL), lambda i: (0, 0))],
        out_specs=row(D_MODEL),
        out_shape=jax.ShapeDtypeStruct((t, D_MODEL), F32),
        compiler_params=_cparams(("parallel",)),
        name="merge_out",
    )(x, oa, ob, gs, w_bf)


def _top16(s):
    n = s.shape[0]
    row = lax.broadcasted_iota(jnp.int32, s.shape, 0).astype(F32)
    rank = jnp.full(s.shape, float(PEER_TOPK), F32)
    vals = []
    for it in range(PEER_TOPK):
        m = jnp.max(s, axis=0, keepdims=True)
        first = jnp.min(jnp.where(s == m, row, float(n)), axis=0, keepdims=True)
        hit = row == first
        rank = jnp.where(hit, float(it), rank)
        s = jnp.where(hit, -jnp.inf, s)
        vals.append(m)
    return jnp.concatenate(vals, axis=0), rank


def _peer_route_kernel(h_ref, g_ref, wq_ref, sk1_ref, sk2_ref,
                       xn_ref, c1_ref, e1_ref, r2_ref, e2_ref):
    xn = _rms_rows(h_ref[...], g_ref[...])
    xn_ref[...] = xn.astype(BF16)
    q = jnp.dot(xn, wq_ref[...], precision=HIGHEST, preferred_element_type=F32)
    nt = (((1,), (1,)), ((), ()))
    for h in range(PEER_HEADS):
        q1 = q[:, h * 2 * PEER_HALF:h * 2 * PEER_HALF + PEER_HALF]
        q2 = q[:, h * 2 * PEER_HALF + PEER_HALF:(h + 1) * 2 * PEER_HALF]
        s1 = lax.dot_general(sk1_ref[h], q1, nt, precision=HIGHEST, preferred_element_type=F32)
        s2 = lax.dot_general(sk2_ref[h], q2, nt, precision=HIGHEST, preferred_element_type=F32)
        v1, rank1 = _top16(s1)
        v2, rank2 = _top16(s2)
        half = PEER_TOPK // 2
        cand = jnp.concatenate([v1[0:1, :] + v2] + [v1[a:a + 1, :] + v2[0:half, :] for a in range(1, half)]
                               + [v1[half:PEER_TOPK, :] + v2[0:1, :]], axis=0)
        seg = [(0, PEER_TOPK)] + [(PEER_TOPK + (a - 1) * half, half) for a in range(1, half)] \
            + [(PEER_TOPK + (half - 1) * half + a, 1) for a in range(half)]
        crow = lax.broadcasted_iota(jnp.int32, cand.shape, 0).astype(F32)
        work = cand
        taken = jnp.zeros(cand.shape, F32)
        for _ in range(PEER_TOPK):
            m = jnp.max(work, axis=0, keepdims=True)
            first = jnp.min(jnp.where(work == m, crow, float(cand.shape[0])), axis=0, keepdims=True)
            hit = crow == first
            taken = jnp.where(hit, 1.0, taken)
            work = jnp.where(hit, -jnp.inf, work)
        top = cand[0:1, :]
        zsum = jnp.sum(taken * jnp.exp(cand - top), axis=0, keepdims=True)
        c1 = jnp.zeros(s1.shape, F32)
        for a, (start, size) in enumerate(seg):
            cnt = jnp.sum(taken[start:start + size, :], axis=0, keepdims=True)
            c1 = jnp.where(rank1 == float(a), cnt, c1)
        c1_ref[h] = c1
        e1_ref[h] = jnp.exp(s1 - v1[0:1, :]) / zsum
        r2_ref[h] = rank2
        e2_ref[h] = jnp.exp(s2 - v2[0:1, :])


def _peer_route(h, g, wq, sk1, sk2, tn, row_off=0):
    t = h.shape[0] - row_off
    blk0 = row_off // tn
    full = lambda a: pl.BlockSpec(a.shape, lambda i: (0,) * a.ndim)
    tab = pl.BlockSpec((PEER_HEADS, PEER_NKEYS, tn), lambda i: (0, 0, i))
    tab_shape = jax.ShapeDtypeStruct((PEER_HEADS, PEER_NKEYS, t), F32)
    return pl.pallas_call(
        _peer_route_kernel,
        grid=(t // tn,),
        in_specs=[pl.BlockSpec((tn, D_MODEL), lambda i: (i + blk0, 0)), full(g), full(wq), full(sk1), full(sk2)],
        out_specs=[pl.BlockSpec((tn, D_MODEL), lambda i: (i, 0)), tab, tab, tab, tab],
        out_shape=[jax.ShapeDtypeStruct((t, D_MODEL), BF16), tab_shape, tab_shape, tab_shape, tab_shape],
        compiler_params=_cparams(("parallel",)),
        name="peer_route",
    )(h, g, wq, sk1, sk2)


def _peer_expert_kernel(h_ref, xn_ref, c1_ref, e1_ref, r2_ref, e2_ref, u_ref, v_ref, o_ref, acc_ref, *, n_i):
    e = pl.program_id(1)

    @pl.when(e == 0)
    def _():
        acc_ref[...] = jnp.zeros_like(acc_ref)

    nt = (((1,), (1,)), ((), ()))
    tn = (((0,), (0,)), ((), ()))
    heads = range(PEER_HEADS)
    xn = xn_ref[...]
    r2 = [r2_ref[h].astype(BF16) for h in heads]
    e2 = [e2_ref[h].astype(BF16) for h in heads]
    rows_per = n_i // PEER_SLICES * PEER_NKEYS
    slices = [slice(s * rows_per, (s + 1) * rows_per) for s in range(PEER_SLICES)]
    ht = [lax.dot_general(u_ref[sl, :], xn, nt, preferred_element_type=F32) for sl in slices]
    gh = []
    for s in range(PEER_SLICES):
        act = 0.5 * ht[s] * (1.0 + lax.erf(ht[s] * (2.0 ** -0.5)))
        rows = []
        for ii in range(s * (n_i // PEER_SLICES), (s + 1) * (n_i // PEER_SLICES)):
            gate = None
            for h in heads:
                c1 = c1_ref[h, ii:ii + 1, :].astype(BF16)
                e1 = e1_ref[h, ii:ii + 1, :].astype(BF16)
                term = jnp.where(r2[h] < c1, e2[h] * e1, jnp.zeros_like(e2[h]))
                gate = term if gate is None else gate + term
            rows.append(gate)
        gh.append(jnp.concatenate(rows, axis=0) * act.astype(BF16))
    out = None
    for s in range(PEER_SLICES):
        d = lax.dot_general(gh[s], v_ref[slices[s], :], tn, preferred_element_type=F32)
        out = d if out is None else out + d
    acc_ref[...] += out

    @pl.when(e == pl.num_programs(1) - 1)
    def _():
        o_ref[...] = h_ref[...] + acc_ref[...]


def _peer_experts(h, xn, c1, e1, r2, e2, u_bf, v_bf, tn, n_i, row_off=0):
    t = xn.shape[0]
    n_exp = u_bf.shape[0]
    eb = n_i * PEER_NKEYS
    blk0 = row_off // tn
    tab_i = pl.BlockSpec((PEER_HEADS, n_i, tn), lambda i, e: (0, e, i))
    tab_j = pl.BlockSpec((PEER_HEADS, PEER_NKEYS, tn), lambda i, e: (0, 0, i))
    return pl.pallas_call(
        functools.partial(_peer_expert_kernel, n_i=n_i),
        grid=(t // tn, n_exp // eb),
        in_specs=[pl.BlockSpec((tn, D_MODEL), lambda i, e: (i + blk0, 0)),
                  pl.BlockSpec((tn, D_MODEL), lambda i, e: (i, 0)),
                  tab_i, tab_i, tab_j, tab_j,
                  pl.BlockSpec((eb, D_MODEL), lambda i, e: (e, 0)),
                  pl.BlockSpec((eb, D_MODEL), lambda i, e: (e, 0))],
        out_specs=pl.BlockSpec((tn, D_MODEL), lambda i, e: (i, 0)),
        out_shape=jax.ShapeDtypeStruct((t, D_MODEL), F32),
        scratch_shapes=[pltpu.VMEM((tn, D_MODEL), F32)],
        compiler_params=_cparams(("parallel", "arbitrary")),
        name="peer_experts",
    )(h, xn, c1, e1, r2, e2, u_bf, v_bf)


PROMPT_PAD = 512 - N_META
ROW_TILE = 256
SCAN_CHUNK = 64
ATT_TQ = 256
ATT_TK = 256
ATT_GROUP = 8
PEER_TN = 512
PEER_NI = 8
PEER_SLICES = 4
SAMPLE_PAGES_PER_STEP = 8


def kernel(x_prompt, x_sample, cache_k, cache_v, page_table, state_shift, state_wkv, meta_tokens, norm_mix_g, w_in, shift_mu, decay_w0, decay_w2, iclr_a0, iclr_a2, gate_g2, k_k, k_a, r_k, gn_w, gn_b, q_norm_g, k_norm_g, sb_bias, w_out, norm_ffn_g, peer_wq, peer_subkeys1, peer_subkeys2, peer_u, peer_v):
    depth = w_in.shape[0]
    assert depth == 1 and x_prompt.shape[0] == 1
    l = 0
    seq = x_prompt.shape[1]
    db, s_new, _ = x_sample.shape
    bd = _seg_ones()
    lw = {'mu': shift_mu[l], 'w0': decay_w0[l], 'w2': decay_w2[l], 'a0': iclr_a0[l], 'a2': iclr_a2[l],
          'g2': gate_g2[l], 'k_k': k_k[l], 'k_a': k_a[l], 'r_k': r_k[l], 'gn_w': gn_w[l], 'gn_b': gn_b[l]}
    g_mix = norm_mix_g[l].reshape(1, D_MODEL)
    g_ffn = norm_ffn_g[l].reshape(1, D_MODEL)
    w_rw = w_in[l][:, :SHIFT_WIDTH].astype(BF16)
    w_sb = w_in[l][:, SHIFT_WIDTH:].astype(BF16)
    qg = jnp.tile(q_norm_g[l], N_HEADS).reshape(1, D_MODEL)
    kg = jnp.tile(k_norm_g[l], N_HEADS).reshape(1, D_MODEL)
    w_o = w_out[l].astype(BF16)
    u_bf = peer_u[l].astype(BF16)
    v_bf = peer_v[l].astype(BF16)

    pad = PROMPT_PAD
    t_all = pad + N_META + seq
    xp = jnp.concatenate([jnp.zeros((pad, D_MODEL), F32), meta_tokens.astype(F32), x_prompt[0]], axis=0)
    p_rw = _proj_rw(xp, g_mix, w_rw, ROW_TILE)
    qb, kn, kb, vv, vb, gs = _proj_sb(xp, g_mix, w_sb, qg, kg, bd, ROW_TILE, LOG2E * HEAD_DIM ** -0.5)
    o_a, wkv_p = _rwkv(p_rw[None], jnp.zeros((1, 1, SHIFT_WIDTH), F32),
                       jnp.zeros((1, N_HEADS, HEAD_DIM, HEAD_DIM), F32), lw, bd, SCAN_CHUNK, SCAN_CHUNK)
    o_b = _sb_prompt(qb, kb, vb, sb_bias[l], ATT_TQ, ATT_TK, pad, ATT_GROUP)
    h_p = _merge(xp, o_a[0], o_b, gs, w_o, ROW_TILE)
    xn_p, c1, e1, r2, e2 = _peer_route(h_p, g_ffn, peer_wq[l], peer_subkeys1[l], peer_subkeys2[l], ROW_TILE,
                                       row_off=pad + N_META)
    y_p = _peer_experts(h_p, xn_p, c1, e1, r2, e2, u_bf, v_bf, PEER_TN, PEER_NI, row_off=pad + N_META)

    k_prompt = kn[pad:].reshape(1, 1, N_META + seq, N_HEADS, HEAD_DIM)
    v_prompt = vv[pad:].reshape(1, 1, N_META + seq, N_HEADS, HEAD_DIM)
    shift_prompt = p_rw[t_all - 1].reshape(1, 1, SHIFT_WIDTH)

    n_s = db * s_new
    xs = x_sample.reshape(n_s, D_MODEL)
    ps_rw = _proj_rw(xs, g_mix, w_rw, n_s)
    qs, kns, _, vs, _, gss = _proj_sb(xs, g_mix, w_sb, qg, kg, bd, n_s, HEAD_DIM ** -0.5)
    ps_pad = jnp.pad(ps_rw.reshape(db, s_new, SHIFT_WIDTH), ((0, 0), (0, SCAN_CHUNK - s_new), (0, 0)))
    o_as, wkv_s = _rwkv(ps_pad, state_shift[l][:, None, :], state_wkv[l], lw, bd, SCAN_CHUNK, s_new)
    o_bs = _sb_sample(qs.astype(F32).reshape(db, s_new, D_MODEL), kns.reshape(db, s_new, D_MODEL),
                      vs.reshape(db, s_new, D_MODEL), cache_k[l], cache_v[l], page_table, sb_bias[l])
    h_s = _merge(xs, o_as[:, :s_new].reshape(n_s, D_MODEL), o_bs.reshape(n_s, D_MODEL), gss, w_o, n_s)
    xn_s, c1s, e1s, r2s, e2s = _peer_route(h_s, g_ffn, peer_wq[l], peer_subkeys1[l], peer_subkeys2[l], n_s)
    y_s = _peer_experts(h_s, xn_s, c1s, e1s, r2s, e2s, u_bf, v_bf, n_s, PEER_NI)

    return (y_p[None], y_s.reshape(db, s_new, D_MODEL), k_prompt, v_prompt, shift_prompt, wkv_p[None],
            kns.reshape(1, db, s_new, N_HEADS, HEAD_DIM), vs.reshape(1, db, s_new, N_HEADS, HEAD_DIM),
            ps_rw.reshape(db, s_new, SHIFT_WIDTH)[:, s_new - 1][None], wkv_s[None])
```

```python
import functools

import jax
import jax.numpy as jnp
from jax import lax
from jax.experimental import pallas as pl
from jax.experimental.pallas import tpu as pltpu

F32 = jnp.float32
BF16 = jnp.bfloat16
HIGHEST = lax.Precision.HIGHEST

D_MODEL = 1024
HEAD_DIM = 64
N_HEADS = 16
N_META = 16
RW_WIDTH = 1024
LORA_W = 64
GATE_W = 128
SHIFT_WIDTH = 3 * RW_WIDTH + 2 * LORA_W + GATE_W
NORM_EPS = 1e-6
LOG2E = 1.4426950408889634
GN_EPS = 64e-5
PEER_HEADS = 8
PEER_NKEYS = 128
PEER_HALF = 128
PEER_TOPK = 16
SEG_BLOCK = 256
VMEM_LIMIT = 56 * 1024 * 1024


def _cparams(sem):
    return pltpu.CompilerParams(dimension_semantics=sem, vmem_limit_bytes=VMEM_LIMIT)


def _seg_ones():
    r = jnp.arange(SEG_BLOCK) // HEAD_DIM
    return (r[:, None] == r[None, :]).astype(F32)


def _seg_sum(x, bd):
    parts = []
    for c in range(x.shape[1] // SEG_BLOCK):
        parts.append(jnp.dot(x[:, c * SEG_BLOCK:(c + 1) * SEG_BLOCK], bd, precision=HIGHEST,
                             preferred_element_type=F32))
    return jnp.concatenate(parts, axis=1)


def _split(x):
    hi = x.astype(BF16)
    return hi, (x - hi.astype(F32)).astype(BF16)


def _dot3(a, b, dims):
    dg = functools.partial(lax.dot_general, dimension_numbers=dims, preferred_element_type=F32)
    return dg(a[0], b[0]) + dg(a[0], b[1]) + dg(a[1], b[0])


def _rms_rows(x, g):
    return x * lax.rsqrt(jnp.mean(x * x, axis=-1, keepdims=True) + NORM_EPS) * g


def _softplus(x):
    sign_bit = jnp.uint32(0x80000000)
    neg_abs = lax.bitcast_convert_type(lax.bitcast_convert_type(x, jnp.uint32) | sign_bit, F32)
    return jnp.maximum(x, 0.0) + jnp.log(1.0 + jnp.exp(neg_abs))


def _softplus2(x):
    sign_bit = jnp.uint32(0x80000000)
    neg_abs = lax.bitcast_convert_type(lax.bitcast_convert_type(x, jnp.uint32) | sign_bit, F32)
    return jnp.maximum(x, 0.0) + jnp.log(1.0 + jnp.exp2(neg_abs)) * LOG2E


def _sigmoid(x):
    return 1.0 / (1.0 + jnp.exp(-x))


def _proj_rw_kernel(x_ref, g_ref, w_ref, o_ref):
    xn = _rms_rows(x_ref[...], g_ref[...])
    o_ref[...] = jnp.dot(xn.astype(BF16), w_ref[...], preferred_element_type=F32)


def _proj_rw(x, g, w_bf, tm):
    t = x.shape[0]
    n = w_bf.shape[1]
    return pl.pallas_call(
        _proj_rw_kernel,
        grid=(t // tm,),
        in_specs=[pl.BlockSpec((tm, D_MODEL), lambda i: (i, 0)),
                  pl.BlockSpec((1, D_MODEL), lambda i: (0, 0)),
                  pl.BlockSpec((D_MODEL, n), lambda i: (0, 0))],
        out_specs=pl.BlockSpec((tm, n), lambda i: (i, 0)),
        out_shape=jax.ShapeDtypeStruct((t, n), F32),
        compiler_params=_cparams(("parallel",)),
        name="proj_rw",
    )(x, g, w_bf)


def _proj_sb_kernel(x_ref, g_ref, w_ref, qg_ref, kg_ref, bd_ref,
                    q_ref, k_ref, kb_ref, v_ref, vb_ref, gs_ref, *, q_scale):
    xn = _rms_rows(x_ref[...], g_ref[...]).astype(BF16)
    p = jnp.dot(xn, w_ref[...], preferred_element_type=F32)
    bd = bd_ref[...]
    q = p[:, 0:D_MODEL]
    k = p[:, D_MODEL:2 * D_MODEL]
    v = p[:, 2 * D_MODEL:3 * D_MODEL]
    qn = q * lax.rsqrt(_seg_sum(q * q, bd) * (1.0 / HEAD_DIM) + NORM_EPS) * qg_ref[...]
    kn = k * lax.rsqrt(_seg_sum(k * k, bd) * (1.0 / HEAD_DIM) + NORM_EPS) * kg_ref[...]
    q_ref[...] = (qn * q_scale).astype(BF16)
    k_ref[...] = kn
    kb_ref[...] = kn.astype(BF16)
    v_ref[...] = v
    vb_ref[...] = v.astype(BF16)
    gs_ref[...] = _sigmoid(p[:, 3 * D_MODEL:5 * D_MODEL])


def _proj_sb(x, g, w_bf, qg, kg, bd, tm, q_scale):
    t = x.shape[0]
    n = w_bf.shape[1]
    row = lambda w: pl.BlockSpec((tm, w), lambda i: (i, 0))
    full = lambda a: pl.BlockSpec(a.shape, lambda i: (0,) * a.ndim)
    return pl.pallas_call(
        functools.partial(_proj_sb_kernel, q_scale=q_scale),
        grid=(t // tm,),
        in_specs=[row(D_MODEL), full(g), full(w_bf), full(qg), full(kg), full(bd)],
        out_specs=[row(D_MODEL), row(D_MODEL), row(D_MODEL), row(D_MODEL), row(D_MODEL), row(2 * D_MODEL)],
        out_shape=[jax.ShapeDtypeStruct((t, D_MODEL), BF16), jax.ShapeDtypeStruct((t, D_MODEL), F32),
                   jax.ShapeDtypeStruct((t, D_MODEL), BF16), jax.ShapeDtypeStruct((t, D_MODEL), F32),
                   jax.ShapeDtypeStruct((t, D_MODEL), BF16), jax.ShapeDtypeStruct((t, 2 * D_MODEL), F32)],
        compiler_params=_cparams(("parallel",)),
        name="proj_sb",
    )(x, g, w_bf, qg, kg, bd)


def _rwkv_kernel(p_ref, sh0_ref, s0_ref, mu_ref, w0_ref, w2_ref, a0_ref, a2_ref, g2_ref, kk_ref, ka_ref,
                 rk_ref, gnw_ref, gnb_ref, bd_ref,
                 o_ref, st_out_ref,
                 st_ref, last_ref, y_s, *, n_steps):
    c = pl.program_id(1)
    tc = p_ref.shape[1]

    @pl.when(c == 0)
    def _():
        st_ref[...] = s0_ref[0]
        last_ref[...] = sh0_ref[0]

    p = p_ref[0]
    row = lax.broadcasted_iota(jnp.int32, p.shape, 0)
    prev = jnp.where(row == 0, last_ref[...], pltpu.roll(p, 1, axis=0))
    last_ref[...] = p[n_steps - 1:n_steps, :]
    xs = p + mu_ref[...] * (prev - p)
    r = xs[:, 0:RW_WIDTH]
    k = xs[:, RW_WIDTH:2 * RW_WIDTH]
    v = xs[:, 2 * RW_WIDTH:3 * RW_WIDTH]
    o1 = 3 * RW_WIDTH
    xw = xs[:, o1:o1 + LORA_W]
    xa = xs[:, o1 + LORA_W:o1 + 2 * LORA_W]
    xg = xs[:, o1 + 2 * LORA_W:o1 + 2 * LORA_W + GATE_W]
    bd = bd_ref[...]
    dot_hi = functools.partial(jnp.dot, precision=HIGHEST, preferred_element_type=F32)
    w = -_softplus(-(w0_ref[...] + dot_hi(jnp.tanh(xw), w2_ref[...]))) - 0.5
    logw = -jnp.exp(w)
    a = _sigmoid(a0_ref[...] + dot_hi(xa, a2_ref[...]))
    g = dot_hi(_sigmoid(xg), g2_ref[...])
    kk = k * kk_ref[...]
    kk = kk * lax.rsqrt(jnp.maximum(_seg_sum(kk * kk, bd), 1e-12))
    k = k * (1.0 + (a - 1.0) * ka_ref[...])
    b = kk * a
    k_in, v_in = k, v
    if n_steps < tc:
        live = (lax.broadcasted_iota(jnp.int32, (tc, 1), 0) < n_steps).astype(F32)
        logw, kk, b, k_in, v_in = logw * live, kk * live, b * live, k * live, v * live

    ti = lax.broadcasted_iota(jnp.int32, (tc, tc), 0)
    si = lax.broadcasted_iota(jnp.int32, (tc, tc), 1)
    incl = si <= ti
    strict = si < ti
    eye = (si == ti).astype(F32)
    cl = dot_hi(incl.astype(F32), logw)
    e_pos = jnp.exp(cl)
    e_neg = jnp.exp(-cl)
    at = -(kk * jnp.exp(cl - logw))
    bt = b * e_neg
    kt = k_in * e_neg
    rt = r * e_pos
    w_end = e_pos[tc - 1:tc, :]
    bh = bt * w_end
    kh = kt * w_end
    levels = []
    half = 1
    while half < tc:
        same = (ti >> half.bit_length()) == (si >> half.bit_length())
        levels.append(jnp.where(same, jnp.where((ti & half) != 0, jnp.where((si & half) == 0, 1.0, 0.0), 0.0), 0.0))
        half *= 2
    nt = (((1,), (1,)), ((), ()))
    nn = (((1,), (0,)), ((), ()))
    tn = (((0,), (0,)), ((), ()))
    t2 = lax.broadcasted_iota(jnp.int32, (tc, 2 * tc), 0)
    s2 = lax.broadcasted_iota(jnp.int32, (tc, 2 * tc), 1)
    right_strict = jnp.where(s2 >= tc, jnp.where(s2 - tc < t2, 1.0, 0.0), 0.0)
    both_incl = jnp.where((s2 & (tc - 1)) <= t2, 1.0, 0.0)
    heads = range(N_HEADS)
    sls = [slice(h * HEAD_DIM, (h + 1) * HEAD_DIM) for h in heads]
    s0 = [st_ref[h] for h in heads]
    x2 = [_split(jnp.concatenate([at[:, sl], rt[:, sl]], axis=0)) for sl in sls]
    p = [_dot3(x2[h], _split(jnp.concatenate([bt[:, sls[h]], kt[:, sls[h]]], axis=0)), nt)
         for h in heads]
    n = [jnp.where(strict, p[h][0:tc, 0:tc], 0.0) for h in heads]
    t_inv = [eye + n[h] * levels[0] for h in heads]
    for lv in levels[1:]:
        ts = [_split(t) for t in t_inv]
        x = [_split(_dot3(_split(n[h] * lv), ts[h], nn)) for h in heads]
        t_inv = [t_inv[h] + _dot3(ts[h], x[h], nn) for h in heads]
    ps = [_dot3(x2[h], _split(s0[h]), nt) for h in heads]
    vh = [v_in[:, sl] for sl in sls]
    rhs = [ps[h][0:tc] + _dot3(_split(p[h][0:tc] * right_strict), _split(jnp.concatenate([vh[h], vh[h]], axis=0)), nn)
           for h in heads]
    u = [_dot3(_split(t_inv[h]), _split(rhs[h]), nn) for h in heads]
    uv = [_split(jnp.concatenate([u[h], vh[h]], axis=0)) for h in heads]
    for h in heads:
        y_s[:, sls[h]] = ps[h][tc:2 * tc] + _dot3(_split(p[h][tc:2 * tc] * both_incl), uv[h], nn)
        st_ref[h] = (s0[h] * w_end[:, sls[h]]
                     + _dot3(uv[h], _split(jnp.concatenate([bh[:, sls[h]], kh[:, sls[h]]], axis=0)), tn))

    y = y_s[...]
    mean = _seg_sum(y, bd) * (1.0 / HEAD_DIM)
    yc = y - mean
    var = _seg_sum(yc * yc, bd) * (1.0 / HEAD_DIM)
    y = yc * lax.rsqrt(var + GN_EPS) * gnw_ref[...] + gnb_ref[...]
    y = y + _seg_sum(r * k * rk_ref[...], bd) * v
    o_ref[0] = y * g

    @pl.when(c == pl.num_programs(1) - 1)
    def _():
        st_out_ref[0] = st_ref[...]


def _rwkv(p_rw, shift0, s0, lw, bd, tc, n_steps):
    b, t, _ = p_rw.shape
    nc = t // tc
    assert n_steps == tc or nc == 1
    vec = lambda a: a.reshape(1, -1).astype(F32)
    params = [vec(lw['mu']), vec(lw['w0']), lw['w2'], vec(lw['a0']), lw['a2'], lw['g2'], vec(lw['k_k']),
              vec(lw['k_a']), vec(lw['r_k']), vec(lw['gn_w']), vec(lw['gn_b']), bd]
    full = lambda a: pl.BlockSpec(a.shape, lambda i, c: (0,) * a.ndim)
    return pl.pallas_call(
        functools.partial(_rwkv_kernel, n_steps=n_steps),
        grid=(b, nc),
        in_specs=[pl.BlockSpec((1, tc, SHIFT_WIDTH), lambda i, c: (i, c, 0)),
                  pl.BlockSpec((1, 1, SHIFT_WIDTH), lambda i, c: (i, 0, 0)),
                  pl.BlockSpec((1, N_HEADS, HEAD_DIM, HEAD_DIM), lambda i, c: (i, 0, 0, 0))]
                 + [full(a) for a in params],
        out_specs=[pl.BlockSpec((1, tc, RW_WIDTH), lambda i, c: (i, c, 0)),
                   pl.BlockSpec((1, N_HEADS, HEAD_DIM, HEAD_DIM), lambda i, c: (i, 0, 0, 0))],
        out_shape=[jax.ShapeDtypeStruct((b, t, RW_WIDTH), F32),
                   jax.ShapeDtypeStruct((b, N_HEADS, HEAD_DIM, HEAD_DIM), F32)],
        scratch_shapes=[pltpu.VMEM((N_HEADS, HEAD_DIM, HEAD_DIM), F32),
                        pltpu.VMEM((1, SHIFT_WIDTH), F32), pltpu.VMEM((tc, RW_WIDTH), F32)],
        compiler_params=_cparams(("parallel", "arbitrary")),
        name="rwkv7",
    )(p_rw, shift0, s0, *params)


def _sb_prompt_kernel(q_ref, k_ref, v_ref, b_ref, one_ref, tri_ref, o_ref, acc_ref, km_ref, vm_ref,
                      *, tq, tk, pad, group):
    i = pl.program_id(1)
    q = q_ref[...]
    lane = lax.broadcasted_iota(jnp.int32, (1, 2 * HEAD_DIM), 1)
    halves = (lane < HEAD_DIM, lane >= HEAD_DIM)
    qh = [jnp.where(halves[h], q, b_ref[h]) for h in range(2)]
    kone = [one_ref[h] for h in range(2)]
    tri = tri_ref[...]
    causal = (lax.broadcasted_iota(jnp.int32, (tq, tk), 1) < lax.broadcasted_iota(jnp.int32, (tq, tk), 0))
    nt = (((1,), (1,)), ((), ()))
    acc_ref[...] = jnp.zeros_like(acc_ref)

    @pl.when(i == 0)
    def _():
        for h in range(2):
            km_ref[h] = jnp.where(halves[h], k_ref[...], kone[h])
            vm_ref[h] = jnp.where(halves[h], v_ref[...], jnp.zeros_like(v_ref[...]))

    def tiles(js, carry, diagonal):
        ch = [(t, h) for t in range(len(js)) for h in range(2)]
        offs = [pl.multiple_of(j * tk, tk) for j in js]
        z = [lax.dot_general(qh[h], km_ref[h, pl.ds(offs[t], tk), :], nt, preferred_element_type=F32)
             for t, h in ch]
        sp = [_softplus2(x) for x in z]
        spm = [jnp.where(causal, x, 0.0) for x in sp] if diagonal else sp
        rows = [jnp.sum(x, axis=1, keepdims=True) for x in spm]
        cs = []
        cur = list(carry)
        for t, h in ch:
            cs.append(cur[h])
            cur[h] = cur[h] + rows[2 * t + h]
        after = [jnp.dot(spm[c].astype(BF16), tri, preferred_element_type=F32) + cs[c] for c in range(len(ch))]
        a = [jnp.exp2((z[c] - sp[c]) - after[c]) for c in range(len(ch))]
        if diagonal:
            a = [jnp.where(causal, x, 0.0) for x in a]
        out = None
        for c, (t, h) in enumerate(ch):
            d = jnp.dot(a[c].astype(BF16), vm_ref[h, pl.ds(offs[t], tk), :], preferred_element_type=F32)
            out = d if out is None else out + d
        acc_ref[...] += out
        return tuple(cur)

    zero = jnp.zeros((tq, 1), F32)
    carry = tiles([i], (zero, zero), True)
    n_rest = jnp.maximum(i - pad // tk, 0)
    j_next = i - 1
    for g in sorted({group, 2, 1}, reverse=True):
        n_g = lax.div(n_rest, g)
        carry = lax.fori_loop(
            0, n_g, lambda m, c, g=g, j0=j_next: tiles([j0 - g * m - t for t in range(g)], c, False), carry)
        j_next = j_next - g * n_g
        n_rest = n_rest - g * n_g
    o_ref[...] = acc_ref[...]


def _sb_prompt(qb, kb, vb, bias, tq, tk, pad, group):
    t = qb.shape[0]
    assert tq == tk, "the causal mask is applied on the diagonal tile only"
    n_pairs = N_HEADS // 2
    tri = (jnp.arange(tk)[:, None] > jnp.arange(tk)[None, :]).astype(BF16)
    lane = jnp.arange(2 * HEAD_DIM)
    spare = (((jnp.arange(N_HEADS) % 2 == 0) * HEAD_DIM)[:, None])
    b_hi, b_lo = _split(bias.astype(F32) * LOG2E)
    bias_l = (jnp.where(lane[None, :] == spare, b_hi[:, None], 0)
              + jnp.where(lane[None, :] == spare + 1, b_lo[:, None], 0)).astype(BF16).reshape(N_HEADS, 1, 2 * HEAD_DIM)
    ones_l = ((lane[None, :] == spare[:2]) | (lane[None, :] == spare[:2] + 1)).astype(BF16).reshape(2, 1, 2 * HEAD_DIM)
    return pl.pallas_call(
        functools.partial(_sb_prompt_kernel, tq=tq, tk=tk, pad=pad, group=group),
        grid=(n_pairs, t // tq),
        in_specs=[pl.BlockSpec((tq, 2 * HEAD_DIM), lambda p, i: (i, p)),
                  pl.BlockSpec((t, 2 * HEAD_DIM), lambda p, i: (0, p)),
                  pl.BlockSpec((t, 2 * HEAD_DIM), lambda p, i: (0, p)),
                  pl.BlockSpec((2, 1, 2 * HEAD_DIM), lambda p, i: (p, 0, 0)),
                  pl.BlockSpec((2, 1, 2 * HEAD_DIM), lambda p, i: (0, 0, 0)),
                  pl.BlockSpec((tk, tk), lambda p, i: (0, 0))],
        out_specs=pl.BlockSpec((tq, 2 * HEAD_DIM), lambda p, i: (i, p)),
        out_shape=jax.ShapeDtypeStruct((t, D_MODEL), F32),
        scratch_shapes=[pltpu.VMEM((tq, 2 * HEAD_DIM), F32), pltpu.VMEM((2, t, 2 * HEAD_DIM), BF16),
                        pltpu.VMEM((2, t, 2 * HEAD_DIM), BF16)],
        compiler_params=_cparams(("arbitrary", "arbitrary")),
        name="sb_prompt",
    )(qb, kb, vb, bias_l, ones_l, tri)


def _sb_sample_kernel(pt_ref, q_ref, kn_ref, vn_ref, *rest, n_slots, s_new):
    del pt_ref
    kt_refs, vt_refs = rest[0:n_slots], rest[n_slots:2 * n_slots]
    b_ref, tri_ref, dmask_ref, o_ref, acc_ref, carry_ref = rest[2 * n_slots:]
    step = pl.program_id(1)
    q = q_ref[0]
    bias = b_ref[...]
    tri = tri_ref[...]
    nt = (((1,), (1,)), ((), ()))

    def weights(z, mask):
        sp = [_softplus(x) for x in z]
        spm = sp if mask is None else [jnp.where(mask, x, 0.0) for x in sp]
        carry = carry_ref[...]
        a = []
        for t in range(len(z)):
            after = jnp.dot(spm[t], tri, precision=HIGHEST, preferred_element_type=F32) + carry
            x = jnp.exp((z[t] - sp[t]) - after)
            a.append((x if mask is None else jnp.where(mask, x, 0.0)).astype(BF16))
            carry = carry + jnp.sum(spm[t], axis=1, keepdims=True)
        carry_ref[...] = carry
        return a

    @pl.when(step == 0)
    def _():
        acc_ref[...] = jnp.zeros_like(acc_ref)
        carry_ref[...] = jnp.zeros_like(carry_ref)
        z = lax.dot_general(q, kn_ref[0].astype(BF16), nt, preferred_element_type=F32) + bias
        kidx = lax.broadcasted_iota(jnp.int32, z.shape, 1)
        qidx = lax.broadcasted_iota(jnp.int32, z.shape, 0) // N_HEADS
        a, = weights([z], kidx < qidx)
        acc_ref[...] += jnp.dot(a, vn_ref[0].astype(BF16), preferred_element_type=F32)

    z = [jnp.dot(q, kt_refs[t][0].astype(BF16), preferred_element_type=F32) + bias for t in range(n_slots)]
    a = weights(z, None)
    out = None
    for t in range(n_slots):
        d = lax.dot_general(a[t], vt_refs[t][0].astype(BF16), nt, preferred_element_type=F32)
        out = d if out is None else out + d
    acc_ref[...] += out

    @pl.when(step == pl.num_programs(1) - 1)
    def _():
        sel = acc_ref[...] * dmask_ref[...]
        for qi in range(s_new):
            o_ref[0, qi:qi + 1, :] = jnp.sum(sel[qi * N_HEADS:(qi + 1) * N_HEADS, :], axis=0, keepdims=True)


def _sb_sample(qn, kn, v, cache_k, cache_v, page_table, bias):
    db, s_new, _ = qn.shape
    n_phys, psize = cache_k.shape[0], cache_k.shape[1]
    n_pages = page_table.shape[1]
    n_slots = SAMPLE_PAGES_PER_STEP
    assert n_pages % n_slots == 0
    nrow = s_new * N_HEADS
    kt = cache_k.transpose(0, 2, 3, 1).reshape(n_phys, D_MODEL, psize)
    vt = cache_v.transpose(0, 2, 3, 1).reshape(n_phys, D_MODEL, psize)
    head_of_lane = jnp.arange(D_MODEL) // HEAD_DIM
    row_head = jnp.arange(nrow) % N_HEADS
    dmask = (row_head[:, None] == head_of_lane[None, :]).astype(F32)
    qbd = (jnp.repeat(qn, N_HEADS, axis=1) * dmask[None]).astype(BF16)
    padn = lambda a: jnp.pad(a, ((0, 0), (0, psize - s_new), (0, 0)))
    bias_r = bias.astype(F32)[row_head].reshape(nrow, 1)
    tri = (jnp.arange(psize)[:, None] > jnp.arange(psize)[None, :]).astype(F32)
    const = lambda a: pl.BlockSpec(a.shape, lambda b, j, pt: (0, 0))
    page = lambda s: pl.BlockSpec((1, D_MODEL, psize),
                                  lambda b, j, pt: (pt[b, n_pages - 1 - (j * n_slots + s)], 0, 0))
    grid_spec = pltpu.PrefetchScalarGridSpec(
        num_scalar_prefetch=1,
        grid=(db, n_pages // n_slots),
        in_specs=[pl.BlockSpec((1, nrow, D_MODEL), lambda b, j, pt: (b, 0, 0)),
                  pl.BlockSpec((1, psize, D_MODEL), lambda b, j, pt: (b, 0, 0)),
                  pl.BlockSpec((1, psize, D_MODEL), lambda b, j, pt: (b, 0, 0))]
                 + [page(s) for s in range(n_slots)] * 2
                 + [const(bias_r), const(tri), const(dmask)],
        out_specs=pl.BlockSpec((1, s_new, D_MODEL), lambda b, j, pt: (b, 0, 0)),
        scratch_shapes=[pltpu.VMEM((nrow, D_MODEL), F32), pltpu.VMEM((nrow, 1), F32)],
    )
    return pl.pallas_call(
        functools.partial(_sb_sample_kernel, n_slots=n_slots, s_new=s_new),
        grid_spec=grid_spec,
        out_shape=jax.ShapeDtypeStruct((db, s_new, D_MODEL), F32),
        compiler_params=_cparams(("parallel", "arbitrary")),
        name="sb_sample",
    )(page_table, qbd, padn(kn), padn(v), *([kt] * n_slots), *([vt] * n_slots), bias_r, tri, dmask)


def _merge_kernel(x_ref, oa_ref, ob_ref, gs_ref, w_ref, o_ref):
    gs = gs_ref[...]
    merged = gs[:, 0:D_MODEL] * oa_ref[...] + gs[:, D_MODEL:2 * D_MODEL] * ob_ref[...]
    o_ref[...] = x_ref[...] + jnp.dot(merged.astype(BF16), w_ref[...], preferred_element_type=F32)


def _merge(x, oa, ob, gs, w_bf, tm):
    t = x.shape[0]
    row = lambda w: pl.BlockSpec((tm, w), lambda i: (i, 0))
    return pl.pallas_call(
        _merge_kernel,
        grid=(t // tm,),
        in_specs=[row(D_MODEL), row(D_MODEL), row(D_MODEL), row(2 * D_MODEL),
                  pl.BlockSpec((D_MODEL, D_MODEL), lambda i: (0, 0))],
        out_specs=row(D_MODEL),
        out_shape=jax.ShapeDtypeStruct((t, D_MODEL), F32),
        compiler_params=_cparams(("parallel",)),
        name="merge_out",
    )(x, oa, ob, gs, w_bf)


def _top16(s):
    n = s.shape[0]
    row = lax.broadcasted_iota(jnp.int32, s.shape, 0).astype(F32)
    rank = jnp.full(s.shape, float(PEER_TOPK), F32)
    vals = []
    for it in range(PEER_TOPK):
        m = jnp.max(s, axis=0, keepdims=True)
        first = jnp.min(jnp.where(s == m, row, float(n)), axis=0, keepdims=True)
        hit = row == first
        rank = jnp.where(hit, float(it), rank)
        s = jnp.where(hit, -jnp.inf, s)
        vals.append(m)
    return jnp.concatenate(vals, axis=0), rank


def _peer_route_kernel(h_ref, g_ref, wq_ref, sk1_ref, sk2_ref,
                       xn_ref, c1_ref, e1_ref, r2_ref, e2_ref):
    xn = _rms_rows(h_ref[...], g_ref[...])
    xn_ref[...] = xn.astype(BF16)
    q = jnp.dot(xn, wq_ref[...], precision=HIGHEST, preferred_element_type=F32)
    nt = (((1,), (1,)), ((), ()))
    for h in range(PEER_HEADS):
        q1 = q[:, h * 2 * PEER_HALF:h * 2 * PEER_HALF + PEER_HALF]
        q2 = q[:, h * 2 * PEER_HALF + PEER_HALF:(h + 1) * 2 * PEER_HALF]
        s1 = lax.dot_general(sk1_ref[h], q1, nt, precision=HIGHEST, preferred_element_type=F32)
        s2 = lax.dot_general(sk2_ref[h], q2, nt, precision=HIGHEST, preferred_element_type=F32)
        v1, rank1 = _top16(s1)
        v2, rank2 = _top16(s2)
        half = PEER_TOPK // 2
        cand = jnp.concatenate([v1[0:1, :] + v2] + [v1[a:a + 1, :] + v2[0:half, :] for a in range(1, half)]
                               + [v1[half:PEER_TOPK, :] + v2[0:1, :]], axis=0)
        seg = [(0, PEER_TOPK)] + [(PEER_TOPK + (a - 1) * half, half) for a in range(1, half)] \
            + [(PEER_TOPK + (half - 1) * half + a, 1) for a in range(half)]
        crow = lax.broadcasted_iota(jnp.int32, cand.shape, 0).astype(F32)
        work = cand
        taken = jnp.zeros(cand.shape, F32)
        for _ in range(PEER_TOPK):
            m = jnp.max(work, axis=0, keepdims=True)
            first = jnp.min(jnp.where(work == m, crow, float(cand.shape[0])), axis=0, keepdims=True)
            hit = crow == first
            taken = jnp.where(hit, 1.0, taken)
            work = jnp.where(hit, -jnp.inf, work)
        top = cand[0:1, :]
        zsum = jnp.sum(taken * jnp.exp(cand - top), axis=0, keepdims=True)
        c1 = jnp.zeros(s1.shape, F32)
        for a, (start, size) in enumerate(seg):
            cnt = jnp.sum(taken[start:start + size, :], axis=0, keepdims=True)
            c1 = jnp.where(rank1 == float(a), cnt, c1)
        c1_ref[h] = c1
        e1_ref[h] = jnp.exp(s1 - v1[0:1, :]) / zsum
        r2_ref[h] = rank2
        e2_ref[h] = jnp.exp(s2 - v2[0:1, :])


def _peer_route(h, g, wq, sk1, sk2, tn, row_off=0):
    t = h.shape[0] - row_off
    blk0 = row_off // tn
    full = lambda a: pl.BlockSpec(a.shape, lambda i: (0,) * a.ndim)
    tab = pl.BlockSpec((PEER_HEADS, PEER_NKEYS, tn), lambda i: (0, 0, i))
    tab_shape = jax.ShapeDtypeStruct((PEER_HEADS, PEER_NKEYS, t), F32)
    return pl.pallas_call(
        _peer_route_kernel,
        grid=(t // tn,),
        in_specs=[pl.BlockSpec((tn, D_MODEL), lambda i: (i + blk0, 0)), full(g), full(wq), full(sk1), full(sk2)],
        out_specs=[pl.BlockSpec((tn, D_MODEL), lambda i: (i, 0)), tab, tab, tab, tab],
        out_shape=[jax.ShapeDtypeStruct((t, D_MODEL), BF16), tab_shape, tab_shape, tab_shape, tab_shape],
        compiler_params=_cparams(("parallel",)),
        name="peer_route",
    )(h, g, wq, sk1, sk2)


def _peer_expert_kernel(h_ref, xn_ref, c1_ref, e1_ref, r2_ref, e2_ref, u_ref, v_ref, o_ref, acc_ref, *, n_i):
    e = pl.program_id(1)

    @pl.when(e == 0)
    def _():
        acc_ref[...] = jnp.zeros_like(acc_ref)

    nt = (((1,), (1,)), ((), ()))
    tn = (((0,), (0,)), ((), ()))
    heads = range(PEER_HEADS)
    xn = xn_ref[...]
    r2 = [r2_ref[h].astype(BF16) for h in heads]
    e2 = [e2_ref[h].astype(BF16) for h in heads]
    rows_per = n_i // PEER_SLICES * PEER_NKEYS
    slices = [slice(s * rows_per, (s + 1) * rows_per) for s in range(PEER_SLICES)]
    ht = [lax.dot_general(u_ref[sl, :], xn, nt, preferred_element_type=F32) for sl in slices]
    gh = []
    for s in range(PEER_SLICES):
        act = 0.5 * ht[s] * (1.0 + lax.erf(ht[s] * (2.0 ** -0.5)))
        rows = []
        for ii in range(s * (n_i // PEER_SLICES), (s + 1) * (n_i // PEER_SLICES)):
            gate = None
            for h in heads:
                c1 = c1_ref[h, ii:ii + 1, :].astype(BF16)
                e1 = e1_ref[h, ii:ii + 1, :].astype(BF16)
                term = jnp.where(r2[h] < c1, e2[h] * e1, jnp.zeros_like(e2[h]))
                gate = term if gate is None else gate + term
            rows.append(gate)
        gh.append(jnp.concatenate(rows, axis=0) * act.astype(BF16))
    out = None
    for s in range(PEER_SLICES):
        d = lax.dot_general(gh[s], v_ref[slices[s], :], tn, preferred_element_type=F32)
        out = d if out is None else out + d
    acc_ref[...] += out

    @pl.when(e == pl.num_programs(1) - 1)
    def _():
        o_ref[...] = h_ref[...] + acc_ref[...]


def _peer_experts(h, xn, c1, e1, r2, e2, u_bf, v_bf, tn, n_i, row_off=0):
    t = xn.shape[0]
    n_exp = u_bf.shape[0]
    eb = n_i * PEER_NKEYS
    blk0 = row_off // tn
    tab_i = pl.BlockSpec((PEER_HEADS, n_i, tn), lambda i, e: (0, e, i))
    tab_j = pl.BlockSpec((PEER_HEADS, PEER_NKEYS, tn), lambda i, e: (0, 0, i))
    return pl.pallas_call(
        functools.partial(_peer_expert_kernel, n_i=n_i),
        grid=(t // tn, n_exp // eb),
        in_specs=[pl.BlockSpec((tn, D_MODEL), lambda i, e: (i + blk0, 0)),
                  pl.BlockSpec((tn, D_MODEL), lambda i, e: (i, 0)),
                  tab_i, tab_i, tab_j, tab_j,
                  pl.BlockSpec((eb, D_MODEL), lambda i, e: (e, 0)),
                  pl.BlockSpec((eb, D_MODEL), lambda i, e: (e, 0))],
        out_specs=pl.BlockSpec((tn, D_MODEL), lambda i, e: (i, 0)),
        out_shape=jax.ShapeDtypeStruct((t, D_MODEL), F32),
        scratch_shapes=[pltpu.VMEM((tn, D_MODEL), F32)],
        compiler_params=_cparams(("parallel", "arbitrary")),
        name="peer_experts",
    )(h, xn, c1, e1, r2, e2, u_bf, v_bf)


PROMPT_PAD = 512 - N_META
ROW_TILE = 256
SCAN_CHUNK = 64
ATT_TQ = 256
ATT_TK = 256
ATT_GROUP = 8
PEER_TN = 512
PEER_NI = 8
PEER_SLICES = 4
SAMPLE_PAGES_PER_STEP = 8


def kernel(x_prompt, x_sample, cache_k, cache_v, page_table, state_shift, state_wkv, meta_tokens, norm_mix_g, w_in, shift_mu, decay_w0, decay_w2, iclr_a0, iclr_a2, gate_g2, k_k, k_a, r_k, gn_w, gn_b, q_norm_g, k_norm_g, sb_bias, w_out, norm_ffn_g, peer_wq, peer_subkeys1, peer_subkeys2, peer_u, peer_v):
    depth = w_in.shape[0]
    assert depth == 1 and x_prompt.shape[0] == 1
    l = 0
    seq = x_prompt.shape[1]
    db, s_new, _ = x_sample.shape
    bd = _seg_ones()
    lw = {'mu': shift_mu[l], 'w0': decay_w0[l], 'w2': decay_w2[l], 'a0': iclr_a0[l], 'a2': iclr_a2[l],
          'g2': gate_g2[l], 'k_k': k_k[l], 'k_a': k_a[l], 'r_k': r_k[l], 'gn_w': gn_w[l], 'gn_b': gn_b[l]}
    g_mix = norm_mix_g[l].reshape(1, D_MODEL)
    g_ffn = norm_ffn_g[l].reshape(1, D_MODEL)
    w_rw = w_in[l][:, :SHIFT_WIDTH].astype(BF16)
    w_sb = w_in[l][:, SHIFT_WIDTH:].astype(BF16)
    qg = jnp.tile(q_norm_g[l], N_HEADS).reshape(1, D_MODEL)
    kg = jnp.tile(k_norm_g[l], N_HEADS).reshape(1, D_MODEL)
    w_o = w_out[l].astype(BF16)
    u_bf = peer_u[l].astype(BF16)
    v_bf = peer_v[l].astype(BF16)

    pad = PROMPT_PAD
    t_all = pad + N_META + seq
    xp = jnp.concatenate([jnp.zeros((pad, D_MODEL), F32), meta_tokens.astype(F32), x_prompt[0]], axis=0)
    p_rw = _proj_rw(xp, g_mix, w_rw, ROW_TILE)
    qb, kn, kb, vv, vb, gs = _proj_sb(xp, g_mix, w_sb, qg, kg, bd, ROW_TILE, LOG2E * HEAD_DIM ** -0.5)
    o_a, wkv_p = _rwkv(p_rw[None], jnp.zeros((1, 1, SHIFT_WIDTH), F32),
                       jnp.zeros((1, N_HEADS, HEAD_DIM, HEAD_DIM), F32), lw, bd, SCAN_CHUNK, SCAN_CHUNK)
    o_b = _sb_prompt(qb, kb, vb, sb_bias[l], ATT_TQ, ATT_TK, pad, ATT_GROUP)
    h_p = _merge(xp, o_a[0], o_b, gs, w_o, ROW_TILE)
    xn_p, c1, e1, r2, e2 = _peer_route(h_p, g_ffn, peer_wq[l], peer_subkeys1[l], peer_subkeys2[l], ROW_TILE,
                                       row_off=pad + N_META)
    y_p = _peer_experts(h_p, xn_p, c1, e1, r2, e2, u_bf, v_bf, PEER_TN, PEER_NI, row_off=pad + N_META)

    k_prompt = kn[pad:].reshape(1, 1, N_META + seq, N_HEADS, HEAD_DIM)
    v_prompt = vv[pad:].reshape(1, 1, N_META + seq, N_HEADS, HEAD_DIM)
    shift_prompt = p_rw[t_all - 1].reshape(1, 1, SHIFT_WIDTH)

    n_s = db * s_new
    xs = x_sample.reshape(n_s, D_MODEL)
    ps_rw = _proj_rw(xs, g_mix, w_rw, n_s)
    qs, kns, _, vs, _, gss = _proj_sb(xs, g_mix, w_sb, qg, kg, bd, n_s, HEAD_DIM ** -0.5)
    ps_pad = jnp.pad(ps_rw.reshape(db, s_new, SHIFT_WIDTH), ((0, 0), (0, SCAN_CHUNK - s_new), (0, 0)))
    o_as, wkv_s = _rwkv(ps_pad, state_shift[l][:, None, :], state_wkv[l], lw, bd, SCAN_CHUNK, s_new)
    o_bs = _sb_sample(qs.astype(F32).reshape(db, s_new, D_MODEL), kns.reshape(db, s_new, D_MODEL),
                      vs.reshape(db, s_new, D_MODEL), cache_k[l], cache_v[l], page_table, sb_bias[l])
    h_s = _merge(xs, o_as[:, :s_new].reshape(n_s, D_MODEL), o_bs.reshape(n_s, D_MODEL), gss, w_o, n_s)
    xn_s, c1s, e1s, r2s, e2s = _peer_route(h_s, g_ffn, peer_wq[l], peer_subkeys1[l], peer_subkeys2[l], n_s)
    y_s = _peer_experts(h_s, xn_s, c1s, e1s, r2s, e2s, u_bf, v_bf, n_s, PEER_NI)

    return (y_p[None], y_s.reshape(db, s_new, D_MODEL), k_prompt, v_prompt, shift_prompt, wkv_p[None],
            kns.reshape(1, db, s_new, N_HEADS, HEAD_DIM), vs.reshape(1, db, s_new, N_HEADS, HEAD_DIM),
            ps_rw.reshape(db, s_new, SHIFT_WIDTH)[:, s_new - 1][None], wkv_s[None])
```
